```python
import jax
import jax.numpy as jnp
from jax import lax
import numpy as np

D_MODEL = 1024
BATCH = 32
SEQ = 2048
DEPTH = 1

CHUNK = 64
Q_BLOCK = 128
PLE_DIM = 256
EPS = 1e-6
ROPE_THETA = 500000.0

RET_HEADS = 8
RET_DK = 64
RET_DV = 64
RET_THETA = 10000.0
RET_W = RET_HEADS * RET_DV

DSA_HEADS = 8
DSA_HEAD_DIM = 64
DSA_ROPE = DSA_HEAD_DIM // 4
DSA_NOPE = DSA_HEAD_DIM - DSA_ROPE
DSA_LATENT = 128
DSA_VDIM = 64
DSA_W = DSA_HEADS * DSA_VDIM
TOPK_MAX = 256

IDX_HEADS = 8
IDX_DIM = 32
IDX_ROPE = IDX_DIM // 4

MIX_W = RET_W + DSA_W

IN_SPLITS = (RET_HEADS * RET_DK, RET_HEADS * RET_DK, RET_W, RET_W,
             DSA_HEADS * DSA_HEAD_DIM, DSA_LATENT, DSA_ROPE, DSA_W,
             IDX_HEADS * IDX_DIM, IDX_DIM, IDX_HEADS)
IN_WIDTH = sum(IN_SPLITS)

kernel_name = "hybrid_retention_dsa_block"


def rms_norm(x, g):
    xf = x.astype(jnp.float32)
    y = xf * lax.rsqrt(jnp.mean(xf * xf, -1, keepdims=True) + EPS)
    return (y * g.astype(jnp.float32)).astype(x.dtype)


def head_group_norm(o, g):
    of = o.astype(jnp.float32)
    mu = jnp.mean(of, -1, keepdims=True)
    var = jnp.mean(jnp.square(of - mu), -1, keepdims=True)
    y = ((of - mu) * lax.rsqrt(var + EPS)).reshape(o.shape[:-2] + (-1,))
    return (y * g.astype(jnp.float32)).astype(o.dtype)


def rotary(x, pos, theta):
    r = x.shape[-1]
    inv = theta ** (-jnp.arange(0, r, 2, dtype=jnp.float32) / r)
    ang = pos.astype(jnp.float32)[:, None] * inv[None, :]
    shape = (ang.shape[0],) + (1,) * (x.ndim - 3) + (r // 2,)
    cos = jnp.cos(ang).reshape(shape)
    sin = jnp.sin(ang).reshape(shape)
    xf = x.astype(jnp.float32)
    x1, x2 = xf[..., : r // 2], xf[..., r // 2:]
    out = jnp.concatenate([x1 * cos - x2 * sin, x2 * cos + x1 * sin], -1)
    return out.astype(x.dtype)


def partial_rotary(x, pos, n_rot, theta):
    return jnp.concatenate([rotary(x[..., :n_rot], pos, theta), x[..., n_rot:]], -1)


def retention(q, k, v):
    B, S, H, dk = q.shape
    dv = v.shape[-1]
    nC = S // CHUNK
    dt = q.dtype
    log_g = jnp.log1p(-jnp.exp2(-5.0 - jnp.arange(H, dtype=jnp.float32)))
    n = jnp.arange(CHUNK, dtype=jnp.float32)
    intra = jnp.exp(log_g[:, None, None] * jnp.abs(n[:, None] - n[None, :]))
    zeta = jnp.exp(log_g[:, None] * (CHUNK - 1 - n)[None, :])
    xi = jnp.exp(log_g[:, None] * (n + 1.0)[None, :])
    chunk_decay = jnp.exp(log_g * CHUNK)[:, None, None]

    qc = q.reshape(B, nC, CHUNK, H, dk) * (dk ** -0.5)
    kc = k.reshape(B, nC, CHUNK, H, dk)
    vc = v.reshape(B, nC, CHUNK, H, dv)

    scores = jnp.einsum('bcnhd,bcmhd->bchnm', qc, kc) * intra.astype(dt)
    o_intra = jnp.einsum('bchnm,bcmhe->bcnhe', scores, vc)

    kv = jnp.einsum('bcmhd,bcmhe,hm->cbhde', kc, vc, zeta.astype(dt)).astype(jnp.float32)

    def step(state, kv_c):
        return chunk_decay * state + kv_c, state

    _, prev = lax.scan(step, jnp.zeros((B, H, dk, dv), jnp.float32), kv)
    o_cross = jnp.einsum('bcnhd,cbhde,hn->bcnhe', qc, prev.astype(dt), xi.astype(dt))
    return (o_intra + o_cross).reshape(B, S, H, dv)


def sparse_attention(q_nope, q_pe, c_kv, k_pe, q_idx, k_idx, w_idx, w_uk, w_uv):
    B, S = c_kv.shape[:2]
    n_sel = min(TOPK_MAX, S // 4)
    n_blk = S // Q_BLOCK
    q_lat = jnp.einsum('bshn,hnc->bshc', q_nope, w_uk)
    key_chunk = jnp.arange(S) // CHUNK
    scale = DSA_HEAD_DIM ** -0.5
    idx_scale = IDX_DIM ** -0.5
    gather = jax.vmap(lambda table, ids: table[ids])

    def to_blocks(a):
        return jnp.moveaxis(a.reshape((B, n_blk, Q_BLOCK) + a.shape[2:]), 1, 0)

    def block(args):
        blk, ql, qp, qi, wi = args
        t = blk * Q_BLOCK + jnp.arange(Q_BLOCK)
        valid = key_chunk[None, :] <= (t // CHUNK)[:, None]
        rel = jax.nn.relu(jnp.einsum('bthd,bsd->bths', qi, k_idx) * idx_scale)
        score = jnp.einsum('bth,bths->bts', wi, rel).astype(jnp.float32)
        score = jnp.where(valid[None], score, -jnp.inf)
        top_val, top_idx = lax.top_k(score, n_sel)
        sel_ok = jnp.isfinite(top_val)
        c_sel = gather(c_kv, top_idx)
        pe_sel = gather(k_pe, top_idx)
        logits = (jnp.einsum('bthc,btkc->bthk', ql, c_sel)
                  + jnp.einsum('bthr,btkr->bthk', qp, pe_sel)).astype(jnp.float32) * scale
        logits = jnp.where(sel_ok[:, :, None, :], logits, -jnp.inf)
        prob = jax.nn.softmax(logits, axis=-1).astype(c_kv.dtype)
        o_lat = jnp.einsum('bthk,btkc->bthc', prob, c_sel)
        return jnp.einsum('bthc,hcd->bthd', o_lat, w_uv)

    out = lax.map(block, (jnp.arange(n_blk), to_blocks(q_lat), to_blocks(q_pe),
                          to_blocks(q_idx), to_blocks(w_idx)))
    return jnp.moveaxis(out, 0, 1).reshape(B, S, DSA_HEADS, DSA_VDIM)


def setup_inputs(seed: int = 0) -> dict:
    key = jax.random.key(seed)
    ks = jax.random.split(key, 13)
    f32 = jnp.float32

    def normal(k, shape, scale):
        return jax.random.normal(k, shape, f32) * scale

    def gain(k, shape):
        return 1.0 + 0.01 * jax.random.normal(k, shape, f32)

    return {
        "x": normal(ks[0], (BATCH, SEQ, D_MODEL), 1.0),
        "p": normal(ks[1], (DEPTH, BATCH, SEQ, PLE_DIM), 1.0),
        "positions": jnp.arange(SEQ, dtype=jnp.int32),
        "norm_mix": gain(ks[2], (DEPTH, D_MODEL)),
        "w_in": normal(ks[3], (DEPTH, D_MODEL, IN_WIDTH), D_MODEL ** -0.5),
        "ret_norm": gain(ks[4], (DEPTH, RET_W)),
        "kv_norm": gain(ks[5], (DEPTH, DSA_LATENT)),
        "w_uk": normal(ks[6], (DEPTH, DSA_HEADS, DSA_NOPE, DSA_LATENT), DSA_NOPE ** -0.5),
        "w_uv": normal(ks[7], (DEPTH, DSA_HEADS, DSA_LATENT, DSA_VDIM), DSA_LATENT ** -0.5),
        "w_out": normal(ks[8], (DEPTH, MIX_W, D_MODEL), MIX_W ** -0.5),
        "norm_ple": gain(ks[9], (DEPTH, D_MODEL)),
        "w_ple_gate": normal(ks[10], (DEPTH, D_MODEL, D_MODEL), D_MODEL ** -0.5),
        "w_ple_proj": normal(ks[11], (DEPTH, PLE_DIM, D_MODEL), PLE_DIM ** -0.5),
        "norm_final": gain(ks[12], (D_MODEL,)),
    }


def reference(x, p, positions, norm_mix, w_in, ret_norm, kv_norm, w_uk, w_uv, w_out,
              norm_ple, w_ple_gate, w_ple_proj, norm_final):
    B, S, _ = x.shape
    split_at = np.cumsum(IN_SPLITS)[:-1].tolist()
    for i in range(DEPTH):
        h = rms_norm(x, norm_mix[i])
        proj = h @ w_in[i]
        (r_q, r_k, r_v, r_g, d_q, d_c, d_pe, d_g,
         i_q, i_k, i_w) = jnp.split(proj, split_at, axis=-1)

        r_q = rotary(r_q.reshape(B, S, RET_HEADS, RET_DK), positions, RET_THETA)
        r_k = rotary(r_k.reshape(B, S, RET_HEADS, RET_DK), positions, RET_THETA)
        r_v = r_v.reshape(B, S, RET_HEADS, RET_DV)
        ret = head_group_norm(retention(r_q, r_k, r_v), ret_norm[i]) * jax.nn.silu(r_g)

        d_q = d_q.reshape(B, S, DSA_HEADS, DSA_HEAD_DIM)
        q_pe = rotary(d_q[..., :DSA_ROPE], positions, ROPE_THETA)
        q_nope = d_q[..., DSA_ROPE:]
        c_kv = rms_norm(d_c, kv_norm[i])
        k_pe = rotary(d_pe, positions, ROPE_THETA)
        q_idx = partial_rotary(i_q.reshape(B, S, IDX_HEADS, IDX_DIM), positions, IDX_ROPE, ROPE_THETA)
        k_idx = partial_rotary(i_k, positions, IDX_ROPE, ROPE_THETA)
        w_idx = i_w * (IDX_HEADS ** -0.5)
        dsa = sparse_attention(q_nope, q_pe, c_kv, k_pe, q_idx, k_idx, w_idx,
                               w_uk[i], w_uv[i]).reshape(B, S, DSA_W) * jax.nn.silu(d_g)

        x = x + jnp.concatenate([ret, dsa], axis=-1) @ w_out[i]

        gate = jax.nn.sigmoid(rms_norm(x, norm_ple[i]) @ w_ple_gate[i])
        x = x + gate * (p[i] @ w_ple_proj[i])
    return rms_norm(x, norm_final)
```

```python
import functools
import math

import numpy as np
import jax
import jax.numpy as jnp
from jax import lax
from jax.experimental import pallas as pl
from jax.experimental.pallas import tpu as pltpu

F32 = jnp.float32
BF16 = jnp.bfloat16
I32 = jnp.int32

LANES = 128
SUBLANES = 8
VMEM_LIMIT_BYTES = 48 * 1024 * 1024

CHUNK = 64
Q_BLOCK = 128
EPS = 1e-6
ROPE_THETA = 500000.0
RET_HEADS = 8
RET_DK = 64
RET_THETA = 10000.0
DSA_HEADS = 8
DSA_HEAD_DIM = 64
DSA_ROPE = 16
DSA_LATENT = 128
DSA_VDIM = 64
TOPK_MAX = 256
IDX_HEADS = 8
IDX_DIM = 32
IDX_ROPE = 8
IN_SPLITS = (512, 512, 512, 512, 512, 128, 16, 512, 256, 32, 8)

KPE_LANE = 0
KIDX_LANE = 32
KEY_W = 2 * LANES

INT_MIN = int(np.iinfo(np.int32).min)
INT_MAX = int(np.iinfo(np.int32).max)
NEG_INF_KEY = INT_MIN + 0x7FFFFF

NT_DIMS = (((1,), (1,)), ((), ()))


def _dot(a, b):
    return jnp.dot(a, b, preferred_element_type=F32)


def _dot_nt(a, b):
    return lax.dot_general(a, b, NT_DIMS, preferred_element_type=F32)


def _silu(y):
    return y / (1.0 + jnp.exp(-y))


def _in_proj_kernel(x_ref, g_ref, wn_ref, wt_ref, kvg_ref, kvgc_ref,
                    cosr_ref, sinr_ref, coss_ref, sins_ref,
                    cosit_ref, sinit_ref, cosqt_ref, sinqt_ref,
                    rq_ref, rk_ref, rv_ref, rg_ref, dg_ref, keys_ref,
                    ckvt_ref, iqt_ref, dqt_ref, iwt_ref):
    x = x_ref[0]
    tm = x.shape[0]
    ms = jnp.mean(x * x, axis=-1, keepdims=True)
    h = (x * lax.rsqrt(ms + EPS) * g_ref[...]).astype(BF16)

    def nat(c0, width):
        return _dot(h, wn_ref[:, c0:c0 + width])

    cosr = cosr_ref[...]
    sinr = sinr_ref[...]
    lane = lax.broadcasted_iota(I32, (tm, LANES), 1)
    first_half = (lane & (RET_DK // 2)) == 0

    def rope64(y):
        outs = []
        for g in range(y.shape[1] // LANES):
            yg = y[:, LANES * g:LANES * (g + 1)]
            partner = jnp.where(first_half, pltpu.roll(yg, LANES - RET_DK // 2, 1),
                                pltpu.roll(yg, RET_DK // 2, 1))
            outs.append(yg * cosr + partner * sinr)
        return jnp.concatenate(outs, axis=1)

    rw = RET_HEADS * RET_DK
    rq_ref[0] = (rope64(nat(0, rw)) * (RET_DK ** -0.5)).astype(BF16)
    rk_ref[0] = rope64(nat(rw, rw)).astype(BF16)
    rv_ref[0] = nat(2 * rw, rw).astype(BF16)
    rg_ref[0] = _silu(nat(3 * rw, rw)).astype(BF16)
    dg_ref[0] = _silu(nat(4 * rw, rw)).astype(BF16)
    c0 = 5 * rw
    dc = nat(c0, DSA_LATENT)
    ckv = dc * lax.rsqrt(jnp.mean(dc * dc, axis=-1, keepdims=True) + EPS) * kvg_ref[...]
    keys_ref[0, :, 0:LANES] = ckv.astype(BF16)
    small = nat(c0 + LANES, LANES) * coss_ref[...] + nat(c0 + 2 * LANES, LANES) * sins_ref[...]
    keys_ref[0, :, LANES:KEY_W] = small.astype(BF16)

    r = _dot_nt(wt_ref[...], h)
    cit = cosit_ref[...]
    sit = sinit_ref[...]
    pieces = []
    for hh in range(IDX_HEADS):
        base = IDX_DIM * hh
        xr = r[base:base + IDX_ROPE]
        pieces.append(xr * cit + pltpu.roll(xr, IDX_ROPE // 2, 0) * sit)
        pieces.append(r[base + IDX_ROPE:base + IDX_DIM])
    iqt_ref[0] = jnp.concatenate(pieces, axis=0).astype(BF16)

    cqt = cosqt_ref[...]
    sqt = sinqt_ref[...]
    half = DSA_ROPE // 2
    off = IDX_HEADS * IDX_DIM
    pieces = []
    for hh in range(DSA_HEADS):
        base = off + DSA_HEAD_DIM * hh
        x1 = r[base:base + half]
        x2 = r[base + half:base + DSA_ROPE]
        pieces.append(x1 * cqt - x2 * sqt)
        pieces.append(x2 * cqt + x1 * sqt)
        pieces.append(r[base + DSA_ROPE:base + DSA_HEAD_DIM])
    dqt_ref[0] = jnp.concatenate(pieces, axis=0).astype(BF16)

    off += DSA_HEADS * DSA_HEAD_DIM
    dct = r[off:off + DSA_LATENT]
    ckvt = dct * lax.rsqrt(jnp.mean(dct * dct, axis=0, keepdims=True) + EPS) * kvgc_ref[...]
    ckvt_ref[0] = ckvt.astype(BF16)
    off += DSA_LATENT
    iwt_ref[0] = r[off:off + IDX_HEADS] * (IDX_HEADS ** -0.5 * IDX_DIM ** -0.5)


def _rope_angles(positions, r, theta):
    inv = theta ** (-jnp.arange(0, r, 2, dtype=F32) / r)
    return positions.astype(F32)[:, None] * inv[None, :]


def _in_proj(x, positions, norm_mix, w_in, kv_norm, tm):
    B, S, D = x.shape
    offs = np.cumsum((0,) + IN_SPLITS)
    w_rq, w_rk, w_rv, w_rg, w_dq, w_dc, w_dpe, w_dg, w_iq, w_ik, w_iw = (
        w_in[:, offs[i]:offs[i + 1]] for i in range(len(IN_SPLITS)))

    hp, hi = DSA_ROPE // 2, IDX_ROPE // 2
    small_a = jnp.zeros((D, LANES), F32)
    small_a = small_a.at[:, KPE_LANE:KPE_LANE + DSA_ROPE].set(w_dpe)
    small_a = small_a.at[:, KIDX_LANE:KIDX_LANE + IDX_DIM].set(w_ik)
    small_b = jnp.zeros((D, LANES), F32)
    small_b = small_b.at[:, KPE_LANE:KPE_LANE + hp].set(-w_dpe[:, hp:])
    small_b = small_b.at[:, KPE_LANE + hp:KPE_LANE + DSA_ROPE].set(w_dpe[:, :hp])
    small_b = small_b.at[:, KIDX_LANE:KIDX_LANE + hi].set(-w_ik[:, hi:IDX_ROPE])
    small_b = small_b.at[:, KIDX_LANE + hi:KIDX_LANE + IDX_ROPE].set(w_ik[:, :hi])
    wn = jnp.concatenate([w_rq, w_rk, w_rv, w_rg, w_dg, w_dc, small_a, small_b], axis=1).astype(BF16)
    wt = jnp.concatenate([w_iq.T, w_dq.T, w_dc.T, w_iw.T, jnp.zeros((SUBLANES, D), F32)], axis=0).astype(BF16)
    n_nat = wn.shape[1]
    n_t = wt.shape[0]

    ang_r = _rope_angles(positions, RET_DK, RET_THETA)
    cos_r = jnp.tile(jnp.cos(ang_r), (1, 4))
    sin_r = jnp.tile(jnp.concatenate([-jnp.sin(ang_r), jnp.sin(ang_r)], axis=1), (1, 2))
    ang_p = _rope_angles(positions, DSA_ROPE, ROPE_THETA)
    ang_i = _rope_angles(positions, IDX_ROPE, ROPE_THETA)
    cos_s = jnp.ones((S, LANES), F32)
    cos_s = cos_s.at[:, KPE_LANE:KPE_LANE + DSA_ROPE].set(jnp.tile(jnp.cos(ang_p), (1, 2)))
    cos_s = cos_s.at[:, KIDX_LANE:KIDX_LANE + IDX_ROPE].set(jnp.tile(jnp.cos(ang_i), (1, 2)))
    sin_s = jnp.zeros((S, LANES), F32)
    sin_s = sin_s.at[:, KPE_LANE:KPE_LANE + DSA_ROPE].set(jnp.tile(jnp.sin(ang_p), (1, 2)))
    sin_s = sin_s.at[:, KIDX_LANE:KIDX_LANE + IDX_ROPE].set(jnp.tile(jnp.sin(ang_i), (1, 2)))
    cos_it = jnp.tile(jnp.cos(ang_i).T, (2, 1))
    sin_it = jnp.concatenate([-jnp.sin(ang_i).T, jnp.sin(ang_i).T], axis=0)
    cos_qt = jnp.cos(ang_p).T
    sin_qt = jnp.sin(ang_p).T

    rw = RET_HEADS * RET_DK
    row = lambda b, s: (b, s, 0)
    col = lambda b, s: (b, 0, s)
    full2 = lambda b, s: (0, 0)
    tok_tab = pl.BlockSpec((tm, LANES), lambda b, s: (s, 0))
    col_tab = pl.BlockSpec((SUBLANES, tm), lambda b, s: (0, s))
    out_shape = (
        jax.ShapeDtypeStruct((B, S, rw), BF16),
        jax.ShapeDtypeStruct((B, S, rw), BF16),
        jax.ShapeDtypeStruct((B, S, rw), BF16),
        jax.ShapeDtypeStruct((B, S, rw), BF16),
        jax.ShapeDtypeStruct((B, S, rw), BF16),
        jax.ShapeDtypeStruct((B, S, KEY_W), BF16),
        jax.ShapeDtypeStruct((B, DSA_LATENT, S), BF16),
        jax.ShapeDtypeStruct((B, IDX_HEADS * IDX_DIM, S), BF16),
        jax.ShapeDtypeStruct((B, DSA_HEADS * DSA_HEAD_DIM, S), BF16),
        jax.ShapeDtypeStruct((B, IDX_HEADS, S), F32),
    )
    out_specs = (
        pl.BlockSpec((1, tm, rw), row), pl.BlockSpec((1, tm, rw), row), pl.BlockSpec((1, tm, rw), row),
        pl.BlockSpec((1, tm, rw), row), pl.BlockSpec((1, tm, rw), row),
        pl.BlockSpec((1, tm, KEY_W), row),
        pl.BlockSpec((1, DSA_LATENT, tm), col),
        pl.BlockSpec((1, IDX_HEADS * IDX_DIM, tm), col),
        pl.BlockSpec((1, DSA_HEADS * DSA_HEAD_DIM, tm), col),
        pl.BlockSpec((1, IDX_HEADS, tm), col),
    )
    in_specs = [
        pl.BlockSpec((1, tm, D), row),
        pl.BlockSpec((1, D), full2),
        pl.BlockSpec((D, n_nat), full2),
        pl.BlockSpec((n_t, D), full2),
        pl.BlockSpec((1, DSA_LATENT), full2),
        pl.BlockSpec((DSA_LATENT, 1), full2),
        tok_tab, tok_tab, tok_tab, tok_tab, col_tab, col_tab, col_tab, col_tab,
    ]
    return pl.pallas_call(
        _in_proj_kernel,
        grid=(B, S // tm),
        in_specs=in_specs,
        out_specs=out_specs,
        out_shape=out_shape,
        compiler_params=pltpu.CompilerParams(
            dimension_semantics=("arbitrary", "arbitrary"), vmem_limit_bytes=VMEM_LIMIT_BYTES),
        name="in_proj",
    )(x, norm_mix.reshape(1, D), wn, wt, kv_norm.reshape(1, DSA_LATENT), kv_norm.reshape(DSA_LATENT, 1),
      cos_r, sin_r, cos_s, sin_s, cos_it, sin_it, cos_qt, sin_qt)


def _retention_kernel(q_ref, k_ref, v_ref, g_ref, dmat_ref, xi_ref, zeta_ref, gl_ref, nrm_ref,
                      o_ref, state_ref, *, n_blocks, blk):
    lane = lax.broadcasted_iota(I32, (blk, LANES), 1)
    lo = lane < RET_DK
    r2 = lax.broadcasted_iota(I32, (LANES, LANES), 0) < RET_DK
    c2 = lax.broadcasted_iota(I32, (LANES, LANES), 1) < RET_DK
    same_head = r2 == c2
    state_ref[...] = jnp.zeros((LANES, LANES), F32)
    inv_dv = 1.0 / RET_DK

    def body(i, carry):
        r0 = pl.multiple_of(i * blk, blk)
        q = q_ref[0, pl.ds(r0, blk), :]
        k = k_ref[0, pl.ds(r0, blk), :]
        v = v_ref[0, pl.ds(r0, blk), :]
        qf = q.astype(F32)
        q0 = jnp.where(lo, qf, 0.0).astype(BF16)
        q1 = jnp.where(lo, 0.0, qf).astype(BF16)
        p0 = (_dot_nt(q0, k) * dmat_ref[0, 0]).astype(BF16)
        p1 = (_dot_nt(q1, k) * dmat_ref[0, 1]).astype(BF16)
        o_intra = jnp.where(lo, _dot(p0, v), _dot(p1, v))
        st = state_ref[...]
        o_cross = _dot((qf * xi_ref[0]).astype(BF16), st.astype(BF16))
        o = o_intra + o_cross
        kzt = (k.astype(F32) * zeta_ref[0]).T.astype(BF16)
        state_ref[...] = st * gl_ref[0] + jnp.where(same_head, _dot(kzt, v), 0.0)

        s_lo = jnp.sum(jnp.where(lo, o, 0.0), axis=-1, keepdims=True)
        s_hi = jnp.sum(jnp.where(lo, 0.0, o), axis=-1, keepdims=True)
        d = o - jnp.where(lo, s_lo, s_hi) * inv_dv
        d2 = d * d
        v_lo = jnp.sum(jnp.where(lo, d2, 0.0), axis=-1, keepdims=True)
        v_hi = jnp.sum(jnp.where(lo, 0.0, d2), axis=-1, keepdims=True)
        var = jnp.where(lo, v_lo, v_hi) * inv_dv
        y = d * lax.rsqrt(var + EPS) * nrm_ref[0] * g_ref[0, pl.ds(r0, blk), :].astype(F32)
        o_ref[0, pl.ds(r0, blk), :] = y.astype(BF16)
        return carry

    lax.fori_loop(0, n_blocks, body, 0)


def _retention_tables(blk):
    hs = np.arange(RET_HEADS, dtype=np.float64)
    log_g = np.log1p(-np.exp2(-5.0 - hs))
    t = np.arange(blk, dtype=np.float64)
    dist = np.abs(t[:, None] - t[None, :])
    causal = (t[None, :] // CHUNK) <= (t[:, None] // CHUNK)
    dmat = np.where(causal[None], np.exp(log_g[:, None, None] * dist[None]), 0.0)
    xi = np.exp(log_g[:, None] * (t + 1.0)[None, :])
    zeta = np.exp(log_g[:, None] * (blk - 1.0 - t)[None, :])
    gl = np.exp(log_g * blk)
    npair = RET_HEADS // 2

    def lanes(a):
        a = a.reshape(npair, 2, blk)
        return np.repeat(np.transpose(a, (0, 2, 1)), RET_DK, axis=2)

    gl_l = np.repeat(gl.reshape(npair, 1, 2), RET_DK, axis=2)
    f = lambda a: jnp.asarray(a, F32)
    return f(dmat.reshape(npair, 2, blk, blk)), f(lanes(xi)), f(lanes(zeta)), f(gl_l)


def _retention(rq, rk, rv, rg, ret_norm, blk):
    B, S, W = rq.shape
    npair = W // LANES
    dmat, xi, zeta, gl = _retention_tables(blk)
    seq = pl.BlockSpec((1, S, LANES), lambda b, p: (b, 0, p))
    tab = pl.BlockSpec((1, blk, LANES), lambda b, p: (p, 0, 0))
    vec = pl.BlockSpec((1, 1, LANES), lambda b, p: (p, 0, 0))
    return pl.pallas_call(
        functools.partial(_retention_kernel, n_blocks=S // blk, blk=blk),
        grid=(B, npair),
        in_specs=[seq, seq, seq, seq,
                  pl.BlockSpec((1, 2, blk, blk), lambda b, p: (p, 0, 0, 0)),
                  tab, tab, vec, vec],
        out_specs=seq,
        out_shape=jax.ShapeDtypeStruct((B, S, W), BF16),
        scratch_shapes=[pltpu.VMEM((LANES, LANES), F32)],
        compiler_params=pltpu.CompilerParams(
            dimension_semantics=("arbitrary", "arbitrary"), vmem_limit_bytes=VMEM_LIMIT_BYTES),
        name="retention",
    )(rq, rk, rv, rg, dmat, xi, zeta, gl, ret_norm.reshape(npair, 1, LANES))


def _dsa_kernel(keys_ref, ckvt_ref, iqt_ref, dqt_ref, iwt_ref, wukt_ref, wuvt_ref, gate_ref,
                o_ref, qit_ref, qt_ref, ks_ref, lg_ref, acc_ref, thr_ref, rcut_ref, *, n_sel, n_qblk):
    j = pl.program_id(1)
    T = Q_BLOCK
    HT = DSA_HEADS * T
    n_tiles = j + 1
    scale = DSA_HEAD_DIM ** -0.5

    qit_ref[...] = jnp.zeros(qit_ref.shape, BF16)
    qt_ref[...] = jnp.zeros(qt_ref.shape, BF16)
    for h in range(DSA_HEADS):
        cols = slice(T * h, T * (h + 1))
        r0 = LANES + KIDX_LANE
        qit_ref[r0:r0 + IDX_DIM, cols] = iqt_ref[0, IDX_DIM * h:IDX_DIM * (h + 1), :]
        dq = dqt_ref[0, DSA_HEAD_DIM * h:DSA_HEAD_DIM * (h + 1), :]
        qlat = _dot(wukt_ref[h], dq) * scale
        qt_ref[0:DSA_LATENT, cols] = qlat.astype(BF16)
        r1 = LANES + KPE_LANE
        qt_ref[r1:r1 + DSA_ROPE, cols] = (dq[0:DSA_ROPE].astype(F32) * scale).astype(BF16)

    w = iwt_ref[0]
    lane = lax.broadcasted_iota(I32, (T, T), 1)
    rowi = lax.broadcasted_iota(I32, (T, T), 0)
    first_chunk_query = lane < CHUNK
    diag_limit = j * T + CHUNK

    def score_tile(kt, carry):
        r0 = pl.multiple_of(kt * T, T)
        s = _dot(keys_ref[0, pl.ds(r0, T), :], qit_ref[...])
        score = jnp.zeros((T, T), F32)
        for h in range(IDX_HEADS):
            score = score + jnp.maximum(s[:, T * h:T * (h + 1)], 0.0) * w[h:h + 1, :]
        bits = lax.bitcast_convert_type(score, I32)
        key = bits ^ ((bits >> 31) & INT_MAX)
        hidden = jnp.where(first_chunk_query, jnp.where(rowi + kt * T >= diag_limit, 1, 0), 0)
        ks_ref[pl.ds(r0, T), :] = jnp.where(hidden > 0, NEG_INF_KEY, key)
        return carry

    lax.fori_loop(0, n_tiles, score_tile, 0)
    ks_ref[pl.ds(pl.multiple_of(n_tiles * T, T), T), :] = jnp.full((T, T), NEG_INF_KEY, I32)

    C2 = 2 * T
    n_chunks = (n_tiles + 1) // 2
    vrows = C2 // SUBLANES

    def count(pred):
        def body(c, acc):
            r0 = pl.multiple_of(c * C2, C2)
            kk = ks_ref[pl.ds(r0, C2), :].reshape(vrows, SUBLANES, T)
            rid = (lax.broadcasted_iota(I32, (vrows, SUBLANES, T), 0) * SUBLANES
                   + lax.broadcasted_iota(I32, (vrows, SUBLANES, T), 1) + c * C2)
            return acc + jnp.sum(pred(kk, rid), axis=0)
        acc = lax.fori_loop(0, n_chunks, body, jnp.zeros((SUBLANES, T), I32))
        return jnp.broadcast_to(jnp.sum(acc, axis=0, keepdims=True), (SUBLANES, T))

    row_bits = int(n_qblk * T + T).bit_length()
    thr_ref[...] = jnp.full((SUBLANES, T), NEG_INF_KEY, I32)
    rcut_ref[...] = jnp.zeros((SUBLANES, T), I32)

    @pl.when(n_tiles * T > n_sel)
    def _select_threshold():
        def bit_step(i, thr):
            cand = thr + lax.shift_left(jnp.int32(1), 31 - i)
            cnt = count(lambda kk, rid: jnp.where(kk >= cand[None], 1, 0))
            return jnp.where(cnt >= n_sel, cand, thr)
        thr = lax.fori_loop(0, 32, bit_step, jnp.full((SUBLANES, T), INT_MIN, I32))
        n_gt = count(lambda kk, rid: jnp.where(kk > thr[None], 1, 0))
        n_eq = count(lambda kk, rid: jnp.where(kk == thr[None], 1, 0))
        need = n_sel - n_gt
        thr_ref[...] = thr
        rcut_ref[...] = jnp.full((SUBLANES, T), INT_MAX, I32)

        @pl.when(jnp.max(jnp.where(n_eq > need, 1, 0)) > 0)
        def _():
            def row_step(i, rc):
                cand = rc + lax.shift_left(jnp.int32(1), row_bits - 1 - i)
                cnt = count(lambda kk, rid: jnp.where(kk == thr[None], jnp.where(rid < cand[None], 1, 0), 0))
                return jnp.where(cnt <= need, cand, rc)
            rcut_ref[...] = lax.fori_loop(0, row_bits, row_step, jnp.zeros((SUBLANES, T), I32))

    ks_thr = thr_ref[...]
    rcut = rcut_ref[...]

    trows = T // SUBLANES

    def logit_tile(kt, mx):
        r0 = pl.multiple_of(kt * T, T)
        l = _dot(keys_ref[0, pl.ds(r0, T), :], qt_ref[...])
        kk = ks_ref[pl.ds(r0, T), :].reshape(trows, SUBLANES, T)
        rid = (lax.broadcasted_iota(I32, (trows, SUBLANES, T), 0) * SUBLANES
               + lax.broadcasted_iota(I32, (trows, SUBLANES, T), 1) + kt * T)
        tie = jnp.where(kk == ks_thr[None], jnp.where(rid < rcut[None], 0.0, -jnp.inf), -jnp.inf)
        bias = jnp.where(kk > ks_thr[None], 0.0, tie).reshape(T, T)
        new = []
        for h in range(DSA_HEADS):
            lm = l[:, T * h:T * (h + 1)] + bias
            lg_ref[pl.ds(r0, T), T * h:T * (h + 1)] = lm
            new.append(jnp.max(lm.reshape(trows, SUBLANES, T), axis=0))
        return jnp.maximum(mx, jnp.concatenate(new, axis=1))

    mx = lax.fori_loop(0, n_tiles, logit_tile, jnp.full((SUBLANES, HT), -jnp.inf, F32))
    mrow = jnp.broadcast_to(jnp.max(mx, axis=0, keepdims=True), (SUBLANES, HT))

    acc_ref[...] = jnp.zeros(acc_ref.shape, F32)

    def pv_tile(kt, ls):
        r0 = pl.multiple_of(kt * T, T)
        p = jnp.exp(lg_ref[pl.ds(r0, T), :].reshape(trows, SUBLANES, HT) - mrow[None])
        acc_ref[...] += _dot(ckvt_ref[0, :, pl.ds(r0, T)], p.reshape(T, HT).astype(BF16))
        return ls + jnp.sum(p, axis=0)

    ls = lax.fori_loop(0, n_tiles, pv_tile, jnp.zeros((SUBLANES, HT), F32))
    linv = 1.0 / jnp.sum(ls, axis=0, keepdims=True)
    olat = (acc_ref[...] * linv).astype(BF16)
    outs = [_dot(wuvt_ref[h], olat[:, T * h:T * (h + 1)]) for h in range(DSA_HEADS)]
    o = jnp.concatenate(outs, axis=0).T
    o_ref[0] = (o * gate_ref[0].astype(F32)).astype(BF16)


def _dsa(keys, ckvt, iqt, dqt, iwt, w_uk, w_uv, dgate):
    B, S, _ = keys.shape
    T = Q_BLOCK
    n_qblk = S // T
    n_sel = min(TOPK_MAX, S // 4)
    H = DSA_HEADS
    wukt = jnp.concatenate([jnp.zeros((H, DSA_LATENT, DSA_ROPE), F32), jnp.swapaxes(w_uk, 1, 2)], axis=2).astype(BF16)
    wuvt = jnp.swapaxes(w_uv, 1, 2).astype(BF16)
    HT = H * T
    per_b = lambda b, j: (b, 0, 0)
    qcol = lambda b, j: (b, 0, j)
    full3 = lambda b, j: (0, 0, 0)
    return pl.pallas_call(
        functools.partial(_dsa_kernel, n_sel=n_sel, n_qblk=n_qblk),
        grid=(B, n_qblk),
        in_specs=[
            pl.BlockSpec((1, S, KEY_W), per_b),
            pl.BlockSpec((1, DSA_LATENT, S), per_b),
            pl.BlockSpec((1, IDX_HEADS * IDX_DIM, T), qcol),
            pl.BlockSpec((1, H * DSA_HEAD_DIM, T), qcol),
            pl.BlockSpec((1, IDX_HEADS, T), qcol),
            pl.BlockSpec((H, DSA_LATENT, DSA_HEAD_DIM), full3),
            pl.BlockSpec((H, DSA_VDIM, DSA_LATENT), full3),
            pl.BlockSpec((1, T, H * DSA_VDIM), lambda b, j: (b, j, 0)),
        ],
        out_specs=pl.BlockSpec((1, T, H * DSA_VDIM), lambda b, j: (b, j, 0)),
        out_shape=jax.ShapeDtypeStruct((B, S, H * DSA_VDIM), BF16),
        scratch_shapes=[
            pltpu.VMEM((KEY_W, HT), BF16),
            pltpu.VMEM((KEY_W, HT), BF16),
            pltpu.VMEM((S + T, T), I32),
            pltpu.VMEM((S, HT), F32),
            pltpu.VMEM((DSA_LATENT, HT), F32),
            pltpu.VMEM((SUBLANES, T), I32),
            pltpu.VMEM((SUBLANES, T), I32),
        ],
        compiler_params=pltpu.CompilerParams(
            dimension_semantics=("arbitrary", "arbitrary"), vmem_limit_bytes=VMEM_LIMIT_BYTES),
        name="dsa",
    )(keys, ckvt, iqt, dqt, iwt, wukt, wuvt, dgate)


def _out_kernel(x_ref, ret_ref, dsa_ref, p_ref, wo_ref, gp_ref, wg_ref, wp_ref, gf_ref, o_ref, *, final_norm):
    nr = ret_ref.shape[2]
    x1 = x_ref[0] + _dot(ret_ref[0], wo_ref[0:nr, :]) + _dot(dsa_ref[0], wo_ref[nr:, :])
    ms = jnp.mean(x1 * x1, axis=-1, keepdims=True)
    h = (x1 * lax.rsqrt(ms + EPS) * gp_ref[...]).astype(BF16)
    z = _dot(h, wg_ref[...])
    gate = 1.0 / (1.0 + jnp.exp(-z))
    x2 = x1 + gate * _dot(p_ref[0].astype(BF16), wp_ref[...])
    if final_norm:
        ms2 = jnp.mean(x2 * x2, axis=-1, keepdims=True)
        x2 = x2 * lax.rsqrt(ms2 + EPS) * gf_ref[...]
    o_ref[0] = x2


def _out_proj(x, ret, dsa, p, w_out, norm_ple, w_gate, w_proj, norm_final, tm, final_norm):
    B, S, D = x.shape
    row = lambda b, s: (b, s, 0)
    full2 = lambda b, s: (0, 0)
    return pl.pallas_call(
        functools.partial(_out_kernel, final_norm=final_norm),
        grid=(B, S // tm),
        in_specs=[
            pl.BlockSpec((1, tm, D), row),
            pl.BlockSpec((1, tm, ret.shape[2]), row),
            pl.BlockSpec((1, tm, dsa.shape[2]), row),
            pl.BlockSpec((1, tm, p.shape[2]), row),
            pl.BlockSpec(w_out.shape, full2),
            pl.BlockSpec((1, D), full2),
            pl.BlockSpec(w_gate.shape, full2),
            pl.BlockSpec(w_proj.shape, full2),
            pl.BlockSpec((1, D), full2),
        ],
        out_specs=pl.BlockSpec((1, tm, D), row),
        out_shape=jax.ShapeDtypeStruct((B, S, D), F32),
        compiler_params=pltpu.CompilerParams(
            dimension_semantics=("arbitrary", "arbitrary"), vmem_limit_bytes=VMEM_LIMIT_BYTES),
        name="out_proj",
    )(x, ret, dsa, p, w_out.astype(BF16), norm_ple.reshape(1, D), w_gate.astype(BF16),
      w_proj.astype(BF16), norm_final.reshape(1, D))


def _tiles(S):
    tm = math.gcd(S, 512)
    blk = math.gcd(S, 256)
    return tm, blk


def kernel(x, p, positions, norm_mix, w_in, ret_norm, kv_norm, w_uk, w_uv, w_out, norm_ple, w_ple_gate, w_ple_proj, norm_final):
    B, S, D = x.shape
    depth = w_in.shape[0]
    tm, blk = _tiles(S)
    assert S % Q_BLOCK == 0 and blk % CHUNK == 0
    for i in range(depth):
        rq, rk, rv, rg, dg, keys, ckvt, iqt, dqt, iwt = _in_proj(x, positions, norm_mix[i], w_in[i], kv_norm[i], tm)
        ret = _retention(rq, rk, rv, rg, ret_norm[i], blk)
        dsa = _dsa(keys, ckvt, iqt, dqt, iwt, w_uk[i], w_uv[i], dg)
        x = _out_proj(x, ret, dsa, p[i], w_out[i], norm_ple[i], w_ple_gate[i], w_ple_proj[i],
                      norm_final, tm, final_norm=(i == depth - 1))
    return x
```

```python
import functools
import math

import numpy as np
import jax
import jax.numpy as jnp
from jax import lax
from jax.experimental import pallas as pl
from jax.experimental.pallas import tpu as pltpu

F32 = jnp.float32
BF16 = jnp.bfloat16
I32 = jnp.int32

LANES = 128
SUBLANES = 8
VMEM_LIMIT_BYTES = 48 * 1024 * 1024

CHUNK = 64
Q_BLOCK = 128
EPS = 1e-6
ROPE_THETA = 500000.0
RET_HEADS = 8
RET_DK = 64
RET_THETA = 10000.0
DSA_HEADS = 8
DSA_HEAD_DIM = 64
DSA_ROPE = 16
DSA_LATENT = 128
DSA_VDIM = 64
TOPK_MAX = 256
IDX_HEADS = 8
IDX_DIM = 32
IDX_ROPE = 8
IN_SPLITS = (512, 512, 512, 512, 512, 128, 16, 512, 256, 32, 8)

KPE_LANE = 0
KIDX_LANE = 32
KEY_W = 2 * LANES

INT_MIN = int(np.iinfo(np.int32).min)
INT_MAX = int(np.iinfo(np.int32).max)
NEG_INF_KEY = INT_MIN + 0x7FFFFF

NT_DIMS = (((1,), (1,)), ((), ()))
LOG2E = math.log2(math.e)


def _dot(a, b):
    return jnp.dot(a, b, preferred_element_type=F32)


def _dot_nt(a, b):
    return lax.dot_general(a, b, NT_DIMS, preferred_element_type=F32)


def _silu(y):
    return y / (1.0 + jnp.exp(-y))


def _in_proj_kernel(x_ref, g_ref, wn_ref, wt_ref, kvg_ref, kvgc_ref,
                    cosr_ref, sinr_ref, coss_ref, sins_ref,
                    cosit_ref, sinit_ref, cosqt_ref, sinqt_ref,
                    rq_ref, rk_ref, rv_ref, rg_ref, dg_ref, keys_ref,
                    ckvt_ref, iqt_ref, dqt_ref, iwt_ref):
    x = x_ref[0]
    tm = x.shape[0]
    ms = jnp.mean(x * x, axis=-1, keepdims=True)
    h = (x * lax.rsqrt(ms + EPS) * g_ref[...]).astype(BF16)

    def nat(c0, width):
        return _dot(h, wn_ref[:, c0:c0 + width])

    cosr = cosr_ref[...]
    sinr = sinr_ref[...]
    lane = lax.broadcasted_iota(I32, (tm, LANES), 1)
    first_half = (lane & (RET_DK // 2)) == 0

    def rope64(y):
        outs = []
        for g in range(y.shape[1] // LANES):
            yg = y[:, LANES * g:LANES * (g + 1)]
            partner = jnp.where(first_half, pltpu.roll(yg, LANES - RET_DK // 2, 1),
                                pltpu.roll(yg, RET_DK // 2, 1))
            outs.append(yg * cosr + partner * sinr)
        return jnp.concatenate(outs, axis=1)

    rw = RET_HEADS * RET_DK
    rq_ref[0] = (rope64(nat(0, rw)) * (RET_DK ** -0.5)).astype(BF16)
    rk_ref[0] = rope64(nat(rw, rw)).astype(BF16)
    rv_ref[0] = nat(2 * rw, rw).astype(BF16)
    rg_ref[0] = _silu(nat(3 * rw, rw)).astype(BF16)
    dg_ref[0] = _silu(nat(4 * rw, rw)).astype(BF16)
    c0 = 5 * rw
    dc = nat(c0, DSA_LATENT)
    ckv = dc * lax.rsqrt(jnp.mean(dc * dc, axis=-1, keepdims=True) + EPS) * kvg_ref[...]
    keys_ref[0, :, 0:LANES] = ckv.astype(BF16)
    small = nat(c0 + LANES, LANES) * coss_ref[...] + nat(c0 + 2 * LANES, LANES) * sins_ref[...]
    keys_ref[0, :, LANES:KEY_W] = small.astype(BF16)

    r = _dot_nt(wt_ref[...], h)
    cit = cosit_ref[...]
    sit = sinit_ref[...]
    pieces = []
    for hh in range(IDX_HEADS):
        base = IDX_DIM * hh
        xr = r[base:base + IDX_ROPE]
        pieces.append(xr * cit + pltpu.roll(xr, IDX_ROPE // 2, 0) * sit)
        pieces.append(r[base + IDX_ROPE:base + IDX_DIM])
    iqt_ref[0] = jnp.concatenate(pieces, axis=0).astype(BF16)

    cqt = cosqt_ref[...]
    sqt = sinqt_ref[...]
    half = DSA_ROPE // 2
    off = IDX_HEADS * IDX_DIM
    pieces = []
    for hh in range(DSA_HEADS):
        base = off + DSA_HEAD_DIM * hh
        x1 = r[base:base + half]
        x2 = r[base + half:base + DSA_ROPE]
        pieces.append(x1 * cqt - x2 * sqt)
        pieces.append(x2 * cqt + x1 * sqt)
        pieces.append(r[base + DSA_ROPE:base + DSA_HEAD_DIM])
    dqt_ref[0] = jnp.concatenate(pieces, axis=0).astype(BF16)

    off += DSA_HEADS * DSA_HEAD_DIM
    dct = r[off:off + DSA_LATENT]
    ckvt = dct * lax.rsqrt(jnp.mean(dct * dct, axis=0, keepdims=True) + EPS) * kvgc_ref[...]
    ckvt_ref[0] = ckvt.astype(BF16)
    off += DSA_LATENT
    iwt_ref[0] = r[off:off + IDX_HEADS] * (IDX_HEADS ** -0.5 * IDX_DIM ** -0.5)


def _rope_angles(positions, r, theta):
    inv = theta ** (-jnp.arange(0, r, 2, dtype=F32) / r)
    return positions.astype(F32)[:, None] * inv[None, :]


def _in_proj(x, positions, norm_mix, w_in, kv_norm, tm):
    B, S, D = x.shape
    offs = np.cumsum((0,) + IN_SPLITS)
    w_rq, w_rk, w_rv, w_rg, w_dq, w_dc, w_dpe, w_dg, w_iq, w_ik, w_iw = (
        w_in[:, offs[i]:offs[i + 1]] for i in range(len(IN_SPLITS)))

    hp, hi = DSA_ROPE // 2, IDX_ROPE // 2
    small_a = jnp.zeros((D, LANES), F32)
    small_a = small_a.at[:, KPE_LANE:KPE_LANE + DSA_ROPE].set(w_dpe)
    small_a = small_a.at[:, KIDX_LANE:KIDX_LANE + IDX_DIM].set(w_ik)
    small_b = jnp.zeros((D, LANES), F32)
    small_b = small_b.at[:, KPE_LANE:KPE_LANE + hp].set(-w_dpe[:, hp:])
    small_b = small_b.at[:, KPE_LANE + hp:KPE_LANE + DSA_ROPE].set(w_dpe[:, :hp])
    small_b = small_b.at[:, KIDX_LANE:KIDX_LANE + hi].set(-w_ik[:, hi:IDX_ROPE])
    small_b = small_b.at[:, KIDX_LANE + hi:KIDX_LANE + IDX_ROPE].set(w_ik[:, :hi])
    wn = jnp.concatenate([w_rq, w_rk, w_rv, w_rg, w_dg, w_dc, small_a, small_b], axis=1).astype(BF16)
    wt = jnp.concatenate([w_iq.T, w_dq.T, w_dc.T, w_iw.T, jnp.zeros((SUBLANES, D), F32)], axis=0).astype(BF16)
    n_nat = wn.shape[1]
    n_t = wt.shape[0]

    ang_r = _rope_angles(positions, RET_DK, RET_THETA)
    cos_r = jnp.tile(jnp.cos(ang_r), (1, 4))
    sin_r = jnp.tile(jnp.concatenate([-jnp.sin(ang_r), jnp.sin(ang_r)], axis=1), (1, 2))
    ang_p = _rope_angles(positions, DSA_ROPE, ROPE_THETA)
    ang_i = _rope_angles(positions, IDX_ROPE, ROPE_THETA)
    cos_s = jnp.ones((S, LANES), F32)
    cos_s = cos_s.at[:, KPE_LANE:KPE_LANE + DSA_ROPE].set(jnp.tile(jnp.cos(ang_p), (1, 2)))
    cos_s = cos_s.at[:, KIDX_LANE:KIDX_LANE + IDX_ROPE].set(jnp.tile(jnp.cos(ang_i), (1, 2)))
    sin_s = jnp.zeros((S, LANES), F32)
    sin_s = sin_s.at[:, KPE_LANE:KPE_LANE + DSA_ROPE].set(jnp.tile(jnp.sin(ang_p), (1, 2)))
    sin_s = sin_s.at[:, KIDX_LANE:KIDX_LANE + IDX_ROPE].set(jnp.tile(jnp.sin(ang_i), (1, 2)))
    cos_it = jnp.tile(jnp.cos(ang_i).T, (2, 1))
    sin_it = jnp.concatenate([-jnp.sin(ang_i).T, jnp.sin(ang_i).T], axis=0)
    cos_qt = jnp.cos(ang_p).T
    sin_qt = jnp.sin(ang_p).T

    rw = RET_HEADS * RET_DK
    row = lambda b, s: (b, s, 0)
    col = lambda b, s: (b, 0, s)
    full2 = lambda b, s: (0, 0)
    tok_tab = pl.BlockSpec((tm, LANES), lambda b, s: (s, 0))
    col_tab = pl.BlockSpec((SUBLANES, tm), lambda b, s: (0, s))
    out_shape = (
        jax.ShapeDtypeStruct((B, S, rw), BF16),
        jax.ShapeDtypeStruct((B, S, rw), BF16),
        jax.ShapeDtypeStruct((B, S, rw), BF16),
        jax.ShapeDtypeStruct((B, S, rw), BF16),
        jax.ShapeDtypeStruct((B, S, rw), BF16),
        jax.ShapeDtypeStruct((B, S, KEY_W), BF16),
        jax.ShapeDtypeStruct((B, DSA_LATENT, S), BF16),
        jax.ShapeDtypeStruct((B, IDX_HEADS * IDX_DIM, S), BF16),
        jax.ShapeDtypeStruct((B, DSA_HEADS * DSA_HEAD_DIM, S), BF16),
        jax.ShapeDtypeStruct((B, IDX_HEADS, S), F32),
    )
    out_specs = (
        pl.BlockSpec((1, tm, rw), row), pl.BlockSpec((1, tm, rw), row), pl.BlockSpec((1, tm, rw), row),
        pl.BlockSpec((1, tm, rw), row), pl.BlockSpec((1, tm, rw), row),
        pl.BlockSpec((1, tm, KEY_W), row),
        pl.BlockSpec((1, DSA_LATENT, tm), col),
        pl.BlockSpec((1, IDX_HEADS * IDX_DIM, tm), col),
        pl.BlockSpec((1, DSA_HEADS * DSA_HEAD_DIM, tm), col),
        pl.BlockSpec((1, IDX_HEADS, tm), col),
    )
    in_specs = [
        pl.BlockSpec((1, tm, D), row),
        pl.BlockSpec((1, D), full2),
        pl.BlockSpec((D, n_nat), full2),
        pl.BlockSpec((n_t, D), full2),
        pl.BlockSpec((1, DSA_LATENT), full2),
        pl.BlockSpec((DSA_LATENT, 1), full2),
        tok_tab, tok_tab, tok_tab, tok_tab, col_tab, col_tab, col_tab, col_tab,
    ]
    return pl.pallas_call(
        _in_proj_kernel,
        grid=(B, S // tm),
        in_specs=in_specs,
        out_specs=out_specs,
        out_shape=out_shape,
        compiler_params=pltpu.CompilerParams(
            dimension_semantics=("arbitrary", "arbitrary"), vmem_limit_bytes=VMEM_LIMIT_BYTES),
        name="in_proj",
    )(x, norm_mix.reshape(1, D), wn, wt, kv_norm.reshape(1, DSA_LATENT), kv_norm.reshape(DSA_LATENT, 1),
      cos_r, sin_r, cos_s, sin_s, cos_it, sin_it, cos_qt, sin_qt)


def _retention_kernel(q_ref, k_ref, v_ref, g_ref, dmat_ref, xi_ref, zeta_ref, gl_ref, nrm_ref,
                      o_ref, state_ref, *, n_blocks, blk):
    lane = lax.broadcasted_iota(I32, (blk, LANES), 1)
    lo = lane < RET_DK
    r2 = lax.broadcasted_iota(I32, (LANES, LANES), 0) < RET_DK
    c2 = lax.broadcasted_iota(I32, (LANES, LANES), 1) < RET_DK
    same_head = r2 == c2
    state_ref[...] = jnp.zeros((LANES, LANES), F32)
    inv_dv = 1.0 / RET_DK

    def body(i, carry):
        r0 = pl.multiple_of(i * blk, blk)
        q = q_ref[0, pl.ds(r0, blk), :]
        k = k_ref[0, pl.ds(r0, blk), :]
        v = v_ref[0, pl.ds(r0, blk), :]
        qf = q.astype(F32)
        q0 = jnp.where(lo, qf, 0.0).astype(BF16)
        q1 = jnp.where(lo, 0.0, qf).astype(BF16)
        p0 = (_dot_nt(q0, k) * dmat_ref[0, 0]).astype(BF16)
        p1 = (_dot_nt(q1, k) * dmat_ref[0, 1]).astype(BF16)
        o_intra = jnp.where(lo, _dot(p0, v), _dot(p1, v))
        st = state_ref[...]
        o_cross = _dot((qf * xi_ref[0]).astype(BF16), st.astype(BF16))
        o = o_intra + o_cross
        kzt = (k.astype(F32) * zeta_ref[0]).T.astype(BF16)
        state_ref[...] = st * gl_ref[0] + jnp.where(same_head, _dot(kzt, v), 0.0)

        s_lo = jnp.sum(jnp.where(lo, o, 0.0), axis=-1, keepdims=True)
        s_hi = jnp.sum(jnp.where(lo, 0.0, o), axis=-1, keepdims=True)
        d = o - jnp.where(lo, s_lo, s_hi) * inv_dv
        d2 = d * d
        v_lo = jnp.sum(jnp.where(lo, d2, 0.0), axis=-1, keepdims=True)
        v_hi = jnp.sum(jnp.where(lo, 0.0, d2), axis=-1, keepdims=True)
        var = jnp.where(lo, v_lo, v_hi) * inv_dv
        y = d * lax.rsqrt(var + EPS) * nrm_ref[0] * g_ref[0, pl.ds(r0, blk), :].astype(F32)
        o_ref[0, pl.ds(r0, blk), :] = y.astype(BF16)
        return carry

    lax.fori_loop(0, n_blocks, body, 0)


def _retention_tables(blk):
    hs = np.arange(RET_HEADS, dtype=np.float64)
    log_g = np.log1p(-np.exp2(-5.0 - hs))
    t = np.arange(blk, dtype=np.float64)
    dist = np.abs(t[:, None] - t[None, :])
    causal = (t[None, :] // CHUNK) <= (t[:, None] // CHUNK)
    dmat = np.where(causal[None], np.exp(log_g[:, None, None] * dist[None]), 0.0)
    xi = np.exp(log_g[:, None] * (t + 1.0)[None, :])
    zeta = np.exp(log_g[:, None] * (blk - 1.0 - t)[None, :])
    gl = np.exp(log_g * blk)
    npair = RET_HEADS // 2

    def lanes(a):
        a = a.reshape(npair, 2, blk)
        return np.repeat(np.transpose(a, (0, 2, 1)), RET_DK, axis=2)

    gl_l = np.repeat(gl.reshape(npair, 1, 2), RET_DK, axis=2)
    f = lambda a: jnp.asarray(a, F32)
    return f(dmat.reshape(npair, 2, blk, blk)), f(lanes(xi)), f(lanes(zeta)), f(gl_l)


def _retention(rq, rk, rv, rg, ret_norm, blk):
    B, S, W = rq.shape
    npair = W // LANES
    dmat, xi, zeta, gl = _retention_tables(blk)
    seq = pl.BlockSpec((1, S, LANES), lambda b, p: (b, 0, p))
    tab = pl.BlockSpec((1, blk, LANES), lambda b, p: (p, 0, 0))
    vec = pl.BlockSpec((1, 1, LANES), lambda b, p: (p, 0, 0))
    return pl.pallas_call(
        functools.partial(_retention_kernel, n_blocks=S // blk, blk=blk),
        grid=(B, npair),
        in_specs=[seq, seq, seq, seq,
                  pl.BlockSpec((1, 2, blk, blk), lambda b, p: (p, 0, 0, 0)),
                  tab, tab, vec, vec],
        out_specs=seq,
        out_shape=jax.ShapeDtypeStruct((B, S, W), BF16),
        scratch_shapes=[pltpu.VMEM((LANES, LANES), F32)],
        compiler_params=pltpu.CompilerParams(
            dimension_semantics=("arbitrary", "arbitrary"), vmem_limit_bytes=VMEM_LIMIT_BYTES),
        name="retention",
    )(rq, rk, rv, rg, dmat, xi, zeta, gl, ret_norm.reshape(npair, 1, LANES))


def _for_tile_groups(n_tiles, body):
    n4 = n_tiles >> 2

    def loop_body(c, carry):
        body(c * 4, 4)
        return carry

    lax.fori_loop(0, n4, loop_body, 0)

    @pl.when((n_tiles & 2) != 0)
    def _():
        body(n4 * 4, 2)

    @pl.when((n_tiles & 1) != 0)
    def _():
        body(n4 * 4 + (n_tiles & 2), 1)


def _dsa_kernel(keys_ref, ckvt_ref, iqt_ref, dqt_ref, iwt_ref, wukt_ref, wuvt_ref, gate_ref,
                o_ref, qit_ref, qt_ref, ks_ref, lg_ref, acc_ref, thr_ref, rcut_ref, mx_ref, ls_ref,
                *, n_sel, n_qblk):
    j = pl.program_id(1)
    T = Q_BLOCK
    HT = DSA_HEADS * T
    n_tiles = j + 1
    scale = DSA_HEAD_DIM ** -0.5 * LOG2E

    @pl.when(j == 0)
    def _():
        qit_ref[...] = jnp.zeros(qit_ref.shape, BF16)
        qt_ref[...] = jnp.zeros(qt_ref.shape, BF16)

    for h in range(DSA_HEADS):
        cols = slice(T * h, T * (h + 1))
        r0 = LANES + KIDX_LANE
        qit_ref[r0:r0 + IDX_DIM, cols] = iqt_ref[0, IDX_DIM * h:IDX_DIM * (h + 1), :]
        dq = dqt_ref[0, DSA_HEAD_DIM * h:DSA_HEAD_DIM * (h + 1), :]
        qlat = _dot(wukt_ref[h], dq) * scale
        qt_ref[0:DSA_LATENT, cols] = qlat.astype(BF16)
        r1 = LANES + KPE_LANE
        qt_ref[r1:r1 + DSA_ROPE, cols] = (dq[0:DSA_ROPE].astype(F32) * scale).astype(BF16)

    w = iwt_ref[0]
    lane = lax.broadcasted_iota(I32, (T, T), 1)
    rowi = lax.broadcasted_iota(I32, (T, T), 0)
    first_chunk_query = lane < CHUNK
    diag_limit = j * T + CHUNK

    def score_tiles(first, g):
        for t in range(g):
            kt = first + t
            r0 = pl.multiple_of(kt * T, T)
            s = _dot(keys_ref[0, pl.ds(r0, T), :], qit_ref[...])
            score = jnp.zeros((T, T), F32)
            for h in range(IDX_HEADS):
                score = score + jnp.maximum(s[:, T * h:T * (h + 1)], 0.0) * w[h:h + 1, :]
            bits = lax.bitcast_convert_type(score, I32)
            key = bits ^ ((bits >> 31) & INT_MAX)
            hidden = jnp.where(first_chunk_query, jnp.where(rowi + kt * T >= diag_limit, 1, 0), 0)
            ks_ref[pl.ds(r0, T), :] = jnp.where(hidden > 0, NEG_INF_KEY, key)

    _for_tile_groups(n_tiles, score_tiles)
    ks_ref[pl.ds(pl.multiple_of(n_tiles * T, T), T), :] = jnp.full((T, T), NEG_INF_KEY, I32)

    C2 = 2 * T
    n_chunks = (n_tiles + 1) // 2
    vrows = C2 // SUBLANES

    def count(pred):
        def body(c, acc):
            r0 = pl.multiple_of(c * C2, C2)
            kk = ks_ref[pl.ds(r0, C2), :].reshape(vrows, SUBLANES, T)
            rid = (lax.broadcasted_iota(I32, (vrows, SUBLANES, T), 0) * SUBLANES
                   + lax.broadcasted_iota(I32, (vrows, SUBLANES, T), 1) + c * C2)
            return acc + jnp.sum(pred(kk, rid), axis=0)
        acc = lax.fori_loop(0, n_chunks, body, jnp.zeros((SUBLANES, T), I32))
        return jnp.broadcast_to(jnp.sum(acc, axis=0, keepdims=True), (SUBLANES, T))

    row_bits = int(n_qblk * T + T).bit_length()
    thr_ref[...] = jnp.full((SUBLANES, T), NEG_INF_KEY, I32)
    rcut_ref[...] = jnp.zeros((SUBLANES, T), I32)

    @pl.when(n_tiles * T > n_sel)
    def _select_threshold():
        def bit_step(i, thr):
            cand = thr + lax.shift_left(jnp.int32(1), 31 - i)
            cnt = count(lambda kk, rid: jnp.where(kk >= cand[None], 1, 0))
            return jnp.where(cnt >= n_sel, cand, thr)
        thr = lax.fori_loop(0, 32, bit_step, jnp.full((SUBLANES, T), INT_MIN, I32))
        n_gt = count(lambda kk, rid: jnp.where(kk > thr[None], 1, 0))
        n_eq = count(lambda kk, rid: jnp.where(kk == thr[None], 1, 0))
        need = n_sel - n_gt
        thr_ref[...] = thr
        rcut_ref[...] = jnp.full((SUBLANES, T), INT_MAX, I32)

        @pl.when(jnp.max(jnp.where(n_eq > need, 1, 0)) > 0)
        def _():
            def row_step(i, rc):
                cand = rc + lax.shift_left(jnp.int32(1), row_bits - 1 - i)
                cnt = count(lambda kk, rid: jnp.where(kk == thr[None], jnp.where(rid < cand[None], 1, 0), 0))
                return jnp.where(cnt <= need, cand, rc)
            rcut_ref[...] = lax.fori_loop(0, row_bits, row_step, jnp.zeros((SUBLANES, T), I32))

    ks_thr = thr_ref[...]
    rcut = rcut_ref[...]

    trows = T // SUBLANES

    mx_ref[...] = jnp.full((SUBLANES, HT), -jnp.inf, F32)

    def logit_tiles(first, g):
        mx = mx_ref[...]
        for t in range(g):
            kt = first + t
            r0 = pl.multiple_of(kt * T, T)
            l = _dot(keys_ref[0, pl.ds(r0, T), :], qt_ref[...])
            kk = ks_ref[pl.ds(r0, T), :].reshape(trows, SUBLANES, T)
            rid = (lax.broadcasted_iota(I32, (trows, SUBLANES, T), 0) * SUBLANES
                   + lax.broadcasted_iota(I32, (trows, SUBLANES, T), 1) + kt * T)
            tie = jnp.where(kk == ks_thr[None], jnp.where(rid < rcut[None], 0.0, -jnp.inf), -jnp.inf)
            bias = jnp.where(kk > ks_thr[None], 0.0, tie).reshape(T, T)
            new = []
            for h in range(DSA_HEADS):
                lm = l[:, T * h:T * (h + 1)] + bias
                lg_ref[pl.ds(r0, T), T * h:T * (h + 1)] = lm
                new.append(jnp.max(lm.reshape(trows, SUBLANES, T), axis=0))
            mx = jnp.maximum(mx, jnp.concatenate(new, axis=1))
        mx_ref[...] = mx

    _for_tile_groups(n_tiles, logit_tiles)
    mrow = jnp.broadcast_to(jnp.max(mx_ref[...], axis=0, keepdims=True), (SUBLANES, HT))

    acc_ref[...] = jnp.zeros(acc_ref.shape, F32)
    ls_ref[...] = jnp.zeros((SUBLANES, HT), F32)

    def pv_tiles(first, g):
        r0 = pl.multiple_of(first * T, T)
        ls = ls_ref[...]
        ps = []
        for t in range(g):
            lm = lg_ref[pl.ds(pl.multiple_of((first + t) * T, T), T), :]
            p = jnp.exp2(lm.reshape(trows, SUBLANES, HT) - mrow[None])
            ls = ls + jnp.sum(p, axis=0)
            ps.append(p.reshape(T, HT).astype(BF16))
        p_all = ps[0] if g == 1 else jnp.concatenate(ps, axis=0)
        acc_ref[...] += _dot(ckvt_ref[0, :, pl.ds(r0, g * T)], p_all)
        ls_ref[...] = ls

    _for_tile_groups(n_tiles, pv_tiles)
    linv = 1.0 / jnp.sum(ls_ref[...], axis=0, keepdims=True)
    olat = (acc_ref[...] * linv).astype(BF16)
    outs = [_dot(wuvt_ref[h], olat[:, T * h:T * (h + 1)]) for h in range(DSA_HEADS)]
    o = jnp.concatenate(outs, axis=0).T
    o_ref[0] = (o * gate_ref[0].astype(F32)).astype(BF16)


def _dsa(keys, ckvt, iqt, dqt, iwt, w_uk, w_uv, dgate):
    B, S, _ = keys.shape
    T = Q_BLOCK
    n_qblk = S // T
    n_sel = min(TOPK_MAX, S // 4)
    H = DSA_HEADS
    wukt = jnp.concatenate([jnp.zeros((H, DSA_LATENT, DSA_ROPE), F32), jnp.swapaxes(w_uk, 1, 2)], axis=2).astype(BF16)
    wuvt = jnp.swapaxes(w_uv, 1, 2).astype(BF16)
    HT = H * T
    per_b = lambda b, j: (b, 0, 0)
    qcol = lambda b, j: (b, 0, j)
    full3 = lambda b, j: (0, 0, 0)
    return pl.pallas_call(
        functools.partial(_dsa_kernel, n_sel=n_sel, n_qblk=n_qblk),
        grid=(B, n_qblk),
        in_specs=[
            pl.BlockSpec((1, S, KEY_W), per_b),
            pl.BlockSpec((1, DSA_LATENT, S), per_b),
            pl.BlockSpec((1, IDX_HEADS * IDX_DIM, T), qcol),
            pl.BlockSpec((1, H * DSA_HEAD_DIM, T), qcol),
            pl.BlockSpec((1, IDX_HEADS, T), qcol),
            pl.BlockSpec((H, DSA_LATENT, DSA_HEAD_DIM), full3),
            pl.BlockSpec((H, DSA_VDIM, DSA_LATENT), full3),
            pl.BlockSpec((1, T, H * DSA_VDIM), lambda b, j: (b, j, 0)),
        ],
        out_specs=pl.BlockSpec((1, T, H * DSA_VDIM), lambda b, j: (b, j, 0)),
        out_shape=jax.ShapeDtypeStruct((B, S, H * DSA_VDIM), BF16),
        scratch_shapes=[
            pltpu.VMEM((KEY_W, HT), BF16),
            pltpu.VMEM((KEY_W, HT), BF16),
            pltpu.VMEM((S + T, T), I32),
            pltpu.VMEM((S, HT), F32),
            pltpu.VMEM((DSA_LATENT, HT), F32),
            pltpu.VMEM((SUBLANES, T), I32),
            pltpu.VMEM((SUBLANES, T), I32),
            pltpu.VMEM((SUBLANES, HT), F32),
            pltpu.VMEM((SUBLANES, HT), F32),
        ],
        compiler_params=pltpu.CompilerParams(
            dimension_semantics=("arbitrary", "arbitrary"), vmem_limit_bytes=VMEM_LIMIT_BYTES),
        name="dsa",
    )(keys, ckvt, iqt, dqt, iwt, wukt, wuvt, dgate)


def _out_kernel(x_ref, ret_ref, dsa_ref, p_ref, wo_ref, gp_ref, wg_ref, wp_ref, gf_ref, o_ref, *, final_norm):
    nr = ret_ref.shape[2]
    x1 = x_ref[0] + _dot(ret_ref[0], wo_ref[0:nr, :]) + _dot(dsa_ref[0], wo_ref[nr:, :])
    ms = jnp.mean(x1 * x1, axis=-1, keepdims=True)
    h = (x1 * lax.rsqrt(ms + EPS) * gp_ref[...]).astype(BF16)
    z = _dot(h, wg_ref[...])
    gate = 1.0 / (1.0 + jnp.exp(-z))
    x2 = x1 + gate * _dot(p_ref[0].astype(BF16), wp_ref[...])
    if final_norm:
        ms2 = jnp.mean(x2 * x2, axis=-1, keepdims=True)
        x2 = x2 * lax.rsqrt(ms2 + EPS) * gf_ref[...]
    o_ref[0] = x2


def _out_proj(x, ret, dsa, p, w_out, norm_ple, w_gate, w_proj, norm_final, tm, final_norm):
    B, S, D = x.shape
    row = lambda b, s: (b, s, 0)
    full2 = lambda b, s: (0, 0)
    return pl.pallas_call(
        functools.partial(_out_kernel, final_norm=final_norm),
        grid=(B, S // tm),
        in_specs=[
            pl.BlockSpec((1, tm, D), row),
            pl.BlockSpec((1, tm, ret.shape[2]), row),
            pl.BlockSpec((1, tm, dsa.shape[2]), row),
            pl.BlockSpec((1, tm, p.shape[2]), row),
            pl.BlockSpec(w_out.shape, full2),
            pl.BlockSpec((1, D), full2),
            pl.BlockSpec(w_gate.shape, full2),
            pl.BlockSpec(w_proj.shape, full2),
            pl.BlockSpec((1, D), full2),
        ],
        out_specs=pl.BlockSpec((1, tm, D), row),
        out_shape=jax.ShapeDtypeStruct((B, S, D), F32),
        compiler_params=pltpu.CompilerParams(
            dimension_semantics=("arbitrary", "arbitrary"), vmem_limit_bytes=VMEM_LIMIT_BYTES),
        name="out_proj",
    )(x, ret, dsa, p, w_out.astype(BF16), norm_ple.reshape(1, D), w_gate.astype(BF16),
      w_proj.astype(BF16), norm_final.reshape(1, D))


def _tiles(S):
    tm = math.gcd(S, 512)
    blk = math.gcd(S, 256)
    return tm, blk


def kernel(x, p, positions, norm_mix, w_in, ret_norm, kv_norm, w_uk, w_uv, w_out, norm_ple, w_ple_gate, w_ple_proj, norm_final):
    B, S, D = x.shape
    depth = w_in.shape[0]
    tm, blk = _tiles(S)
    assert S % Q_BLOCK == 0 and blk % CHUNK == 0
    for i in range(depth):
        rq, rk, rv, rg, dg, keys, ckvt, iqt, dqt, iwt = _in_proj(x, positions, norm_mix[i], w_in[i], kv_norm[i], tm)
        ret = _retention(rq, rk, rv, rg, ret_norm[i], blk)
        dsa = _dsa(keys, ckvt, iqt, dqt, iwt, w_uk[i], w_uv[i], dg)
        x = _out_proj(x, ret, dsa, p[i], w_out[i], norm_ple[i], w_ple_gate[i], w_ple_proj[i],
                      norm_final, tm, final_norm=(i == depth - 1))
    return x
```

```python
import functools
import math

import numpy as np
import jax
import jax.numpy as jnp
from jax import lax
from jax.experimental import pallas as pl
from jax.experimental.pallas import tpu as pltpu

F32 = jnp.float32
BF16 = jnp.bfloat16
I32 = jnp.int32

LANES = 128
SUBLANES = 8
VMEM_LIMIT_BYTES = 48 * 1024 * 1024

CHUNK = 64
Q_BLOCK = 128
EPS = 1e-6
ROPE_THETA = 500000.0
RET_HEADS = 8
RET_DK = 64
RET_THETA = 10000.0
DSA_HEADS = 8
DSA_HEAD_DIM = 64
DSA_ROPE = 16
DSA_LATENT = 128
DSA_VDIM = 64
TOPK_MAX = 256
IDX_HEADS = 8
IDX_DIM = 32
IDX_ROPE = 8
IN_SPLITS = (512, 512, 512, 512, 512, 128, 16, 512, 256, 32, 8)

KPE_LANE = 0
KIDX_LANE = 32
KEY_W = 2 * LANES
CKVT_ROWS = DSA_LATENT + 16

INT_MIN = int(np.iinfo(np.int32).min)
INT_MAX = int(np.iinfo(np.int32).max)
NEG_INF_KEY = INT_MIN + 0x7FFFFF

NT_DIMS = (((1,), (1,)), ((), ()))
LOG2E = math.log2(math.e)


def _dot(a, b):
    return jnp.dot(a, b, preferred_element_type=F32)


def _dot_nt(a, b):
    return lax.dot_general(a, b, NT_DIMS, preferred_element_type=F32)


def _silu(y):
    return y / (1.0 + jnp.exp(-y))


def _in_proj_kernel(x_ref, g_ref, wn_ref, wt_ref, kvg_ref, kvgc_ref,
                    cosr_ref, sinr_ref, coss_ref, sins_ref,
                    cosit_ref, sinit_ref, cosqt_ref, sinqt_ref,
                    rq_ref, rk_ref, rv_ref, rg_ref, dg_ref, keys_ref,
                    ckvt_ref, iqt_ref, dqt_ref, iwt_ref):
    x = x_ref[0]
    tm = x.shape[0]
    ms = jnp.mean(x * x, axis=-1, keepdims=True)
    h = (x * lax.rsqrt(ms + EPS) * g_ref[...]).astype(BF16)

    def nat(c0, width):
        return _dot(h, wn_ref[:, c0:c0 + width])

    cosr = cosr_ref[...]
    sinr = sinr_ref[...]
    lane = lax.broadcasted_iota(I32, (tm, LANES), 1)
    first_half = (lane & (RET_DK // 2)) == 0

    def rope64(y):
        outs = []
        for g in range(y.shape[1] // LANES):
            yg = y[:, LANES * g:LANES * (g + 1)]
            partner = jnp.where(first_half, pltpu.roll(yg, LANES - RET_DK // 2, 1),
                                pltpu.roll(yg, RET_DK // 2, 1))
            outs.append(yg * cosr + partner * sinr)
        return jnp.concatenate(outs, axis=1)

    rw = RET_HEADS * RET_DK
    rq_ref[0] = (rope64(nat(0, rw)) * (RET_DK ** -0.5)).astype(BF16)
    rk_ref[0] = rope64(nat(rw, rw)).astype(BF16)
    rv_ref[0] = nat(2 * rw, rw).astype(BF16)
    rg_ref[0] = _silu(nat(3 * rw, rw)).astype(BF16)
    dg_ref[0] = _silu(nat(4 * rw, rw)).astype(BF16)
    c0 = 5 * rw
    dc = nat(c0, DSA_LATENT)
    ckv = dc * lax.rsqrt(jnp.mean(dc * dc, axis=-1, keepdims=True) + EPS) * kvg_ref[...]
    keys_ref[0, :, 0:LANES] = ckv.astype(BF16)
    small = nat(c0 + LANES, LANES) * coss_ref[...] + nat(c0 + 2 * LANES, LANES) * sins_ref[...]
    keys_ref[0, :, LANES:KEY_W] = small.astype(BF16)

    r = _dot_nt(wt_ref[...], h)
    cit = cosit_ref[...]
    sit = sinit_ref[...]
    pieces = []
    for hh in range(IDX_HEADS):
        base = IDX_DIM * hh
        xr = r[base:base + IDX_ROPE]
        pieces.append(xr * cit + pltpu.roll(xr, IDX_ROPE // 2, 0) * sit)
        pieces.append(r[base + IDX_ROPE:base + IDX_DIM])
    iqt_ref[0] = jnp.concatenate(pieces, axis=0).astype(BF16)

    cqt = cosqt_ref[...]
    sqt = sinqt_ref[...]
    half = DSA_ROPE // 2
    off = IDX_HEADS * IDX_DIM
    pieces = []
    for hh in range(DSA_HEADS):
        base = off + DSA_HEAD_DIM * hh
        x1 = r[base:base + half]
        x2 = r[base + half:base + DSA_ROPE]
        pieces.append(x1 * cqt - x2 * sqt)
        pieces.append(x2 * cqt + x1 * sqt)
        pieces.append(r[base + DSA_ROPE:base + DSA_HEAD_DIM])
    dqt_ref[0] = jnp.concatenate(pieces, axis=0).astype(BF16)

    off += DSA_HEADS * DSA_HEAD_DIM
    dct = r[off:off + DSA_LATENT]
    ckvt = dct * lax.rsqrt(jnp.mean(dct * dct, axis=0, keepdims=True) + EPS) * kvgc_ref[...]
    extra = jnp.where(lax.broadcasted_iota(I32, (CKVT_ROWS - DSA_LATENT, tm), 0) == 0, 1.0, 0.0)
    ckvt_ref[0] = jnp.concatenate([ckvt, extra], axis=0).astype(BF16)
    off += DSA_LATENT
    iwt_ref[0] = r[off:off + IDX_HEADS] * (IDX_HEADS ** -0.5 * IDX_DIM ** -0.5)


def _rope_angles(positions, r, theta):
    inv = theta ** (-jnp.arange(0, r, 2, dtype=F32) / r)
    return positions.astype(F32)[:, None] * inv[None, :]


def _in_proj(x, positions, norm_mix, w_in, kv_norm, tm):
    B, S, D = x.shape
    offs = np.cumsum((0,) + IN_SPLITS)
    w_rq, w_rk, w_rv, w_rg, w_dq, w_dc, w_dpe, w_dg, w_iq, w_ik, w_iw = (
        w_in[:, offs[i]:offs[i + 1]] for i in range(len(IN_SPLITS)))

    hp, hi = DSA_ROPE // 2, IDX_ROPE // 2
    small_a = jnp.zeros((D, LANES), F32)
    small_a = small_a.at[:, KPE_LANE:KPE_LANE + DSA_ROPE].set(w_dpe)
    small_a = small_a.at[:, KIDX_LANE:KIDX_LANE + IDX_DIM].set(w_ik)
    small_b = jnp.zeros((D, LANES), F32)
    small_b = small_b.at[:, KPE_LANE:KPE_LANE + hp].set(-w_dpe[:, hp:])
    small_b = small_b.at[:, KPE_LANE + hp:KPE_LANE + DSA_ROPE].set(w_dpe[:, :hp])
    small_b = small_b.at[:, KIDX_LANE:KIDX_LANE + hi].set(-w_ik[:, hi:IDX_ROPE])
    small_b = small_b.at[:, KIDX_LANE + hi:KIDX_LANE + IDX_ROPE].set(w_ik[:, :hi])
    wn = jnp.concatenate([w_rq, w_rk, w_rv, w_rg, w_dg, w_dc, small_a, small_b], axis=1).astype(BF16)
    wt = jnp.concatenate([w_iq.T, w_dq.T, w_dc.T, w_iw.T, jnp.zeros((SUBLANES, D), F32)], axis=0).astype(BF16)
    n_nat = wn.shape[1]
    n_t = wt.shape[0]

    ang_r = _rope_angles(positions, RET_DK, RET_THETA)
    cos_r = jnp.tile(jnp.cos(ang_r), (1, 4))
    sin_r = jnp.tile(jnp.concatenate([-jnp.sin(ang_r), jnp.sin(ang_r)], axis=1), (1, 2))
    ang_p = _rope_angles(positions, DSA_ROPE, ROPE_THETA)
    ang_i = _rope_angles(positions, IDX_ROPE, ROPE_THETA)
    cos_s = jnp.ones((S, LANES), F32)
    cos_s = cos_s.at[:, KPE_LANE:KPE_LANE + DSA_ROPE].set(jnp.tile(jnp.cos(ang_p), (1, 2)))
    cos_s = cos_s.at[:, KIDX_LANE:KIDX_LANE + IDX_ROPE].set(jnp.tile(jnp.cos(ang_i), (1, 2)))
    sin_s = jnp.zeros((S, LANES), F32)
    sin_s = sin_s.at[:, KPE_LANE:KPE_LANE + DSA_ROPE].set(jnp.tile(jnp.sin(ang_p), (1, 2)))
    sin_s = sin_s.at[:, KIDX_LANE:KIDX_LANE + IDX_ROPE].set(jnp.tile(jnp.sin(ang_i), (1, 2)))
    cos_it = jnp.tile(jnp.cos(ang_i).T, (2, 1))
    sin_it = jnp.concatenate([-jnp.sin(ang_i).T, jnp.sin(ang_i).T], axis=0)
    cos_qt = jnp.cos(ang_p).T
    sin_qt = jnp.sin(ang_p).T

    rw = RET_HEADS * RET_DK
    row = lambda b, s: (b, s, 0)
    col = lambda b, s: (b, 0, s)
    full2 = lambda b, s: (0, 0)
    tok_tab = pl.BlockSpec((tm, LANES), lambda b, s: (s, 0))
    col_tab = pl.BlockSpec((SUBLANES, tm), lambda b, s: (0, s))
    out_shape = (
        jax.ShapeDtypeStruct((B, S, rw), BF16),
        jax.ShapeDtypeStruct((B, S, rw), BF16),
        jax.ShapeDtypeStruct((B, S, rw), BF16),
        jax.ShapeDtypeStruct((B, S, rw), BF16),
        jax.ShapeDtypeStruct((B, S, rw), BF16),
        jax.ShapeDtypeStruct((B, S, KEY_W), BF16),
        jax.ShapeDtypeStruct((B, CKVT_ROWS, S), BF16),
        jax.ShapeDtypeStruct((B, IDX_HEADS * IDX_DIM, S), BF16),
        jax.ShapeDtypeStruct((B, DSA_HEADS * DSA_HEAD_DIM, S), BF16),
        jax.ShapeDtypeStruct((B, IDX_HEADS, S), F32),
    )
    out_specs = (
        pl.BlockSpec((1, tm, rw), row), pl.BlockSpec((1, tm, rw), row), pl.BlockSpec((1, tm, rw), row),
        pl.BlockSpec((1, tm, rw), row), pl.BlockSpec((1, tm, rw), row),
        pl.BlockSpec((1, tm, KEY_W), row),
        pl.BlockSpec((1, CKVT_ROWS, tm), col),
        pl.BlockSpec((1, IDX_HEADS * IDX_DIM, tm), col),
        pl.BlockSpec((1, DSA_HEADS * DSA_HEAD_DIM, tm), col),
        pl.BlockSpec((1, IDX_HEADS, tm), col),
    )
    in_specs = [
        pl.BlockSpec((1, tm, D), row),
        pl.BlockSpec((1, D), full2),
        pl.BlockSpec((D, n_nat), full2),
        pl.BlockSpec((n_t, D), full2),
        pl.BlockSpec((1, DSA_LATENT), full2),
        pl.BlockSpec((DSA_LATENT, 1), full2),
        tok_tab, tok_tab, tok_tab, tok_tab, col_tab, col_tab, col_tab, col_tab,
    ]
    return pl.pallas_call(
        _in_proj_kernel,
        grid=(B, S // tm),
        in_specs=in_specs,
        out_specs=out_specs,
        out_shape=out_shape,
        compiler_params=pltpu.CompilerParams(
            dimension_semantics=("arbitrary", "arbitrary"), vmem_limit_bytes=VMEM_LIMIT_BYTES),
        name="in_proj",
    )(x, norm_mix.reshape(1, D), wn, wt, kv_norm.reshape(1, DSA_LATENT), kv_norm.reshape(DSA_LATENT, 1),
      cos_r, sin_r, cos_s, sin_s, cos_it, sin_it, cos_qt, sin_qt)


def _retention_kernel(q_ref, k_ref, v_ref, g_ref, dmat_ref, xi_ref, zeta_ref, gl_ref, nrm_ref,
                      o_ref, state_ref, *, n_blocks, blk):
    lane = lax.broadcasted_iota(I32, (blk, LANES), 1)
    lo = lane < RET_DK
    r2 = lax.broadcasted_iota(I32, (LANES, LANES), 0) < RET_DK
    c2 = lax.broadcasted_iota(I32, (LANES, LANES), 1) < RET_DK
    same_head = r2 == c2
    npair = state_ref.shape[0]
    state_ref[...] = jnp.zeros(state_ref.shape, F32)
    inv_dv = 1.0 / RET_DK

    def body(i, carry):
        r0 = pl.multiple_of(i * blk, blk)
        for p in range(npair):
            cols = slice(LANES * p, LANES * (p + 1))
            q = q_ref[0, pl.ds(r0, blk), cols]
            k = k_ref[0, pl.ds(r0, blk), cols]
            v = v_ref[0, pl.ds(r0, blk), cols]
            qf = q.astype(F32)
            q0 = jnp.where(lo, qf, 0.0).astype(BF16)
            q1 = jnp.where(lo, 0.0, qf).astype(BF16)
            p0 = (_dot_nt(q0, k) * dmat_ref[p, 0]).astype(BF16)
            p1 = (_dot_nt(q1, k) * dmat_ref[p, 1]).astype(BF16)
            o_intra = jnp.where(lo, _dot(p0, v), _dot(p1, v))
            st = state_ref[p]
            o_cross = _dot((qf * xi_ref[p]).astype(BF16), st.astype(BF16))
            o = o_intra + o_cross
            kzt = (k.astype(F32) * zeta_ref[p]).T.astype(BF16)
            state_ref[p] = st * gl_ref[p] + jnp.where(same_head, _dot(kzt, v), 0.0)

            s_lo = jnp.sum(jnp.where(lo, o, 0.0), axis=-1, keepdims=True)
            s_hi = jnp.sum(jnp.where(lo, 0.0, o), axis=-1, keepdims=True)
            d = o - jnp.where(lo, s_lo, s_hi) * inv_dv
            d2 = d * d
            v_lo = jnp.sum(jnp.where(lo, d2, 0.0), axis=-1, keepdims=True)
            v_hi = jnp.sum(jnp.where(lo, 0.0, d2), axis=-1, keepdims=True)
            var = jnp.where(lo, v_lo, v_hi) * inv_dv
            y = d * lax.rsqrt(var + EPS) * nrm_ref[p] * g_ref[0, pl.ds(r0, blk), cols].astype(F32)
            o_ref[0, pl.ds(r0, blk), cols] = y.astype(BF16)
        return carry

    lax.fori_loop(0, n_blocks, body, 0)


def _retention_tables(blk):
    hs = np.arange(RET_HEADS, dtype=np.float64)
    log_g = np.log1p(-np.exp2(-5.0 - hs))
    t = np.arange(blk, dtype=np.float64)
    dist = np.abs(t[:, None] - t[None, :])
    causal = (t[None, :] // CHUNK) <= (t[:, None] // CHUNK)
    dmat = np.where(causal[None], np.exp(log_g[:, None, None] * dist[None]), 0.0)
    xi = np.exp(log_g[:, None] * (t + 1.0)[None, :])
    zeta = np.exp(log_g[:, None] * (blk - 1.0 - t)[None, :])
    gl = np.exp(log_g * blk)
    npair = RET_HEADS // 2

    def lanes(a):
        a = a.reshape(npair, 2, blk)
        return np.repeat(np.transpose(a, (0, 2, 1)), RET_DK, axis=2)

    gl_l = np.repeat(gl.reshape(npair, 1, 2), RET_DK, axis=2)
    f = lambda a: jnp.asarray(a, F32)
    return f(dmat.reshape(npair, 2, blk, blk)), f(lanes(xi)), f(lanes(zeta)), f(gl_l)


def _retention(rq, rk, rv, rg, ret_norm, blk):
    B, S, W = rq.shape
    npair = W // LANES
    dmat, xi, zeta, gl = _retention_tables(blk)
    seq = pl.BlockSpec((1, S, W), lambda b: (b, 0, 0))
    tab = pl.BlockSpec((npair, blk, LANES), lambda b: (0, 0, 0))
    vec = pl.BlockSpec((npair, 1, LANES), lambda b: (0, 0, 0))
    return pl.pallas_call(
        functools.partial(_retention_kernel, n_blocks=S // blk, blk=blk),
        grid=(B,),
        in_specs=[seq, seq, seq, seq,
                  pl.BlockSpec((npair, 2, blk, blk), lambda b: (0, 0, 0, 0)),
                  tab, tab, vec, vec],
        out_specs=seq,
        out_shape=jax.ShapeDtypeStruct((B, S, W), BF16),
        scratch_shapes=[pltpu.VMEM((npair, LANES, LANES), F32)],
        compiler_params=pltpu.CompilerParams(
            dimension_semantics=("arbitrary",), vmem_limit_bytes=VMEM_LIMIT_BYTES),
        name="retention",
    )(rq, rk, rv, rg, dmat, xi, zeta, gl, ret_norm.reshape(npair, 1, LANES))


def _for_tile_groups(n_tiles, body):
    n4 = n_tiles >> 2

    def loop_body(c, carry):
        body(c * 4, 4)
        return carry

    lax.fori_loop(0, n4, loop_body, 0)

    @pl.when((n_tiles & 2) != 0)
    def _():
        body(n4 * 4, 2)

    @pl.when((n_tiles & 1) != 0)
    def _():
        body(n4 * 4 + (n_tiles & 2), 1)


def _dsa_kernel(keys_ref, ckvt_ref, iqt_ref, dqt_ref, iwt_ref, wukt_ref, wuvt_ref, gate_ref,
                o_ref, qit_ref, qt_ref, ks_ref, lg_ref, acc_ref, thr_ref, rcut_ref, cnt_ref, mx_ref,
                *, n_sel, n_qblk):
    j = pl.program_id(1)
    T = Q_BLOCK
    HT = DSA_HEADS * T
    n_tiles = j + 1
    scale = DSA_HEAD_DIM ** -0.5 * LOG2E

    @pl.when(j == 0)
    def _():
        qit_ref[...] = jnp.zeros(qit_ref.shape, BF16)
        qt_ref[...] = jnp.zeros(qt_ref.shape, BF16)

    for h in range(DSA_HEADS):
        cols = slice(T * h, T * (h + 1))
        r0 = LANES + KIDX_LANE
        qit_ref[r0:r0 + IDX_DIM, cols] = iqt_ref[0, IDX_DIM * h:IDX_DIM * (h + 1), :]
        dq = dqt_ref[0, DSA_HEAD_DIM * h:DSA_HEAD_DIM * (h + 1), :]
        qlat = _dot(wukt_ref[h], dq) * scale
        qt_ref[0:DSA_LATENT, cols] = qlat.astype(BF16)
        r1 = LANES + KPE_LANE
        qt_ref[r1:r1 + DSA_ROPE, cols] = (dq[0:DSA_ROPE].astype(F32) * scale).astype(BF16)

    w = iwt_ref[0]

    def score_tiles(first, g):
        for t in range(g):
            kt = first + t
            r0 = pl.multiple_of(kt * T, T)
            s = _dot(keys_ref[0, pl.ds(r0, T), :], qit_ref[...])
            score = jnp.zeros((T, T), F32)
            for h in range(IDX_HEADS):
                score = score + jnp.maximum(s[:, T * h:T * (h + 1)], 0.0) * w[h:h + 1, :]
            bits = lax.bitcast_convert_type(score, I32)
            ks_ref[pl.ds(r0, T), :] = bits ^ ((bits >> 31) & INT_MAX)

    _for_tile_groups(n_tiles, score_tiles)
    r_hid = pl.multiple_of(j * T + CHUNK, CHUNK)
    lane_c = lax.broadcasted_iota(I32, (T - CHUNK, T), 1)
    ks_ref[pl.ds(r_hid, T - CHUNK), :] = jnp.where(lane_c < CHUNK, NEG_INF_KEY, ks_ref[pl.ds(r_hid, T - CHUNK), :])
    ks_ref[pl.ds(pl.multiple_of(n_tiles * T, T), T), :] = jnp.full((T, T), NEG_INF_KEY, I32)

    trows = T // SUBLANES
    full = lambda v: jnp.full((SUBLANES, T), v, I32)
    sub_total = lambda acc: jnp.broadcast_to(jnp.sum(acc, axis=0, keepdims=True), (SUBLANES, T))
    thr_ref[...] = full(NEG_INF_KEY)
    rcut_ref[...] = full(0)
    cnt_ref[...] = full(n_sel)

    def descend(nt):
        def bit_step(i, c):
            thr, cnt_thr = c
            cand = thr + lax.shift_left(jnp.int32(1), 31 - i)
            acc = full(0)
            for t in range(nt):
                kk = ks_ref[T * t:T * (t + 1), :].reshape(trows, SUBLANES, T)
                acc = acc + jnp.sum(jnp.where(kk >= cand[None], 1, 0), axis=0)
            cnt = sub_total(acc)
            take = cnt >= n_sel
            return jnp.where(take, cand, thr), jnp.where(take, cnt, cnt_thr)
        thr, cnt_thr = lax.fori_loop(0, 32, bit_step, (full(INT_MIN), full(nt * T)))
        thr_ref[...] = thr
        cnt_ref[...] = cnt_thr
        rcut_ref[...] = full(INT_MAX)

    for nt in range(n_sel // T + 1, n_qblk + 1):
        pl.when(n_tiles == nt)(functools.partial(descend, nt))

    @pl.when(jnp.max(jnp.where(cnt_ref[...] > n_sel, 1, 0)) > 0)
    def _break_ties():
        C2 = 2 * T
        n_chunks = (n_tiles + 1) // 2
        vrows = C2 // SUBLANES
        row_bits = int(n_qblk * T + T).bit_length()
        thr = thr_ref[...]

        def count(pred):
            def body(c, acc):
                r0 = pl.multiple_of(c * C2, C2)
                kk = ks_ref[pl.ds(r0, C2), :].reshape(vrows, SUBLANES, T)
                rid = (lax.broadcasted_iota(I32, (vrows, SUBLANES, T), 0) * SUBLANES
                       + lax.broadcasted_iota(I32, (vrows, SUBLANES, T), 1) + c * C2)
                return acc + jnp.sum(pred(kk, rid), axis=0)
            return sub_total(lax.fori_loop(0, n_chunks, body, full(0)))

        need = n_sel - count(lambda kk, rid: jnp.where(kk > thr[None], 1, 0))

        def row_step(i, rc):
            cand = rc + lax.shift_left(jnp.int32(1), row_bits - 1 - i)
            cnt = count(lambda kk, rid: jnp.where(kk == thr[None], jnp.where(rid < cand[None], 1, 0), 0))
            return jnp.where(cnt <= need, cand, rc)
        rcut_ref[...] = lax.fori_loop(0, row_bits, row_step, full(0))

    ks_thr = thr_ref[...]
    rcut = rcut_ref[...]

    mx_ref[...] = jnp.full((SUBLANES, HT), -jnp.inf, F32)

    def logit_tiles(first, g):
        mx = mx_ref[...]
        for t in range(g):
            kt = first + t
            r0 = pl.multiple_of(kt * T, T)
            l = _dot(keys_ref[0, pl.ds(r0, T), :], qt_ref[...])
            kk = ks_ref[pl.ds(r0, T), :].reshape(trows, SUBLANES, T)
            rid = (lax.broadcasted_iota(I32, (trows, SUBLANES, T), 0) * SUBLANES
                   + lax.broadcasted_iota(I32, (trows, SUBLANES, T), 1) + kt * T)
            tie = jnp.where(kk == ks_thr[None], jnp.where(rid < rcut[None], 0.0, -jnp.inf), -jnp.inf)
            bias = jnp.where(kk > ks_thr[None], 0.0, tie).reshape(T, T)
            new = []
            for h in range(DSA_HEADS):
                lm = l[:, T * h:T * (h + 1)] + bias
                lg_ref[pl.ds(r0, T), T * h:T * (h + 1)] = lm
                new.append(jnp.max(lm.reshape(trows, SUBLANES, T), axis=0))
            mx = jnp.maximum(mx, jnp.concatenate(new, axis=1))
        mx_ref[...] = mx

    _for_tile_groups(n_tiles, logit_tiles)
    mrow = jnp.broadcast_to(jnp.max(mx_ref[...], axis=0, keepdims=True), (SUBLANES, HT))

    acc_ref[...] = jnp.zeros(acc_ref.shape, F32)

    def pv_tiles(first, g):
        r0 = pl.multiple_of(first * T, T)
        ps = []
        for t in range(g):
            lm = lg_ref[pl.ds(pl.multiple_of((first + t) * T, T), T), :]
            p = jnp.exp2(lm.reshape(trows, SUBLANES, HT) - mrow[None])
            ps.append(p.reshape(T, HT).astype(BF16))
        p_all = ps[0] if g == 1 else jnp.concatenate(ps, axis=0)
        acc_ref[...] += _dot(ckvt_ref[0, :, pl.ds(r0, g * T)], p_all)

    _for_tile_groups(n_tiles, pv_tiles)
    linv = 1.0 / acc_ref[DSA_LATENT:DSA_LATENT + 1, :]
    olat = (acc_ref[0:DSA_LATENT, :] * linv).astype(BF16)
    outs = [_dot(wuvt_ref[h], olat[:, T * h:T * (h + 1)]) for h in range(DSA_HEADS)]
    o = jnp.concatenate(outs, axis=0).T
    o_ref[0] = (o * gate_ref[0].astype(F32)).astype(BF16)


def _dsa(keys, ckvt, iqt, dqt, iwt, w_uk, w_uv, dgate):
    B, S, _ = keys.shape
    T = Q_BLOCK
    n_qblk = S // T
    n_sel = min(TOPK_MAX, S // 4)
    H = DSA_HEADS
    wukt = jnp.concatenate([jnp.zeros((H, DSA_LATENT, DSA_ROPE), F32), jnp.swapaxes(w_uk, 1, 2)], axis=2).astype(BF16)
    wuvt = jnp.swapaxes(w_uv, 1, 2).astype(BF16)
    HT = H * T
    per_b = lambda b, j: (b, 0, 0)
    qcol = lambda b, j: (b, 0, j)
    full3 = lambda b, j: (0, 0, 0)
    return pl.pallas_call(
        functools.partial(_dsa_kernel, n_sel=n_sel, n_qblk=n_qblk),
        grid=(B, n_qblk),
        in_specs=[
            pl.BlockSpec((1, S, KEY_W), per_b),
            pl.BlockSpec((1, CKVT_ROWS, S), per_b),
            pl.BlockSpec((1, IDX_HEADS * IDX_DIM, T), qcol),
            pl.BlockSpec((1, H * DSA_HEAD_DIM, T), qcol),
            pl.BlockSpec((1, IDX_HEADS, T), qcol),
            pl.BlockSpec((H, DSA_LATENT, DSA_HEAD_DIM), full3),
            pl.BlockSpec((H, DSA_VDIM, DSA_LATENT), full3),
            pl.BlockSpec((1, T, H * DSA_VDIM), lambda b, j: (b, j, 0)),
        ],
        out_specs=pl.BlockSpec((1, T, H * DSA_VDIM), lambda b, j: (b, j, 0)),
        out_shape=jax.ShapeDtypeStruct((B, S, H * DSA_VDIM), BF16),
        scratch_shapes=[
            pltpu.VMEM((KEY_W, HT), BF16),
            pltpu.VMEM((KEY_W, HT), BF16),
            pltpu.VMEM((S + T, T), I32),
            pltpu.VMEM((S, HT), F32),
            pltpu.VMEM((CKVT_ROWS, HT), F32),
            pltpu.VMEM((SUBLANES, T), I32),
            pltpu.VMEM((SUBLANES, T), I32),
            pltpu.VMEM((SUBLANES, T), I32),
            pltpu.VMEM((SUBLANES, HT), F32),
        ],
        compiler_params=pltpu.CompilerParams(
            dimension_semantics=("arbitrary", "arbitrary"), vmem_limit_bytes=VMEM_LIMIT_BYTES),
        name="dsa",
    )(keys, ckvt, iqt, dqt, iwt, wukt, wuvt, dgate)


def _out_kernel(x_ref, ret_ref, dsa_ref, p_ref, wo_ref, gp_ref, wg_ref, wp_ref, gf_ref, o_ref, *, final_norm):
    nr = ret_ref.shape[2]
    x1 = x_ref[0] + _dot(ret_ref[0], wo_ref[0:nr, :]) + _dot(dsa_ref[0], wo_ref[nr:, :])
    ms = jnp.mean(x1 * x1, axis=-1, keepdims=True)
    h = (x1 * lax.rsqrt(ms + EPS) * gp_ref[...]).astype(BF16)
    z = _dot(h, wg_ref[...])
    gate = 1.0 / (1.0 + jnp.exp(-z))
    x2 = x1 + gate * _dot(p_ref[0].astype(BF16), wp_ref[...])
    if final_norm:
        ms2 = jnp.mean(x2 * x2, axis=-1, keepdims=True)
        x2 = x2 * lax.rsqrt(ms2 + EPS) * gf_ref[...]
    o_ref[0] = x2


def _out_proj(x, ret, dsa, p, w_out, norm_ple, w_gate, w_proj, norm_final, tm, final_norm):
    B, S, D = x.shape
    row = lambda b, s: (b, s, 0)
    full2 = lambda b, s: (0, 0)
    return pl.pallas_call(
        functools.partial(_out_kernel, final_norm=final_norm),
        grid=(B, S // tm),
        in_specs=[
            pl.BlockSpec((1, tm, D), row),
            pl.BlockSpec((1, tm, ret.shape[2]), row),
            pl.BlockSpec((1, tm, dsa.shape[2]), row),
            pl.BlockSpec((1, tm, p.shape[2]), row),
            pl.BlockSpec(w_out.shape, full2),
            pl.BlockSpec((1, D), full2),
            pl.BlockSpec(w_gate.shape, full2),
            pl.BlockSpec(w_proj.shape, full2),
            pl.BlockSpec((1, D), full2),
        ],
        out_specs=pl.BlockSpec((1, tm, D), row),
        out_shape=jax.ShapeDtypeStruct((B, S, D), F32),
        compiler_params=pltpu.CompilerParams(
            dimension_semantics=("arbitrary", "arbitrary"), vmem_limit_bytes=VMEM_LIMIT_BYTES),
        name="out_proj",
    )(x, ret, dsa, p, w_out.astype(BF16), norm_ple.reshape(1, D), w_gate.astype(BF16),
      w_proj.astype(BF16), norm_final.reshape(1, D))


def _tiles(S):
    tm = math.gcd(S, 512)
    blk = math.gcd(S, 256)
    return tm, blk


def kernel(x, p, positions, norm_mix, w_in, ret_norm, kv_norm, w_uk, w_uv, w_out, norm_ple, w_ple_gate, w_ple_proj, norm_final):
    B, S, D = x.shape
    depth = w_in.shape[0]
    tm, blk = _tiles(S)
    assert S % Q_BLOCK == 0 and blk % CHUNK == 0
    for i in range(depth):
        rq, rk, rv, rg, dg, keys, ckvt, iqt, dqt, iwt = _in_proj(x, positions, norm_mix[i], w_in[i], kv_norm[i], tm)
        ret = _retention(rq, rk, rv, rg, ret_norm[i], blk)
        dsa = _dsa(keys, ckvt, iqt, dqt, iwt, w_uk[i], w_uv[i], dg)
        x = _out_proj(x, ret, dsa, p[i], w_out[i], norm_ple[i], w_ple_gate[i], w_ple_proj[i],
                      norm_final, tm, final_norm=(i == depth - 1))
    return x
```

```python
import functools
import math

import numpy as np
import jax
import jax.numpy as jnp
from jax import lax
from jax.experimental import pallas as pl
from jax.experimental.pallas import tpu as pltpu

F32 = jnp.float32
BF16 = jnp.bfloat16
I32 = jnp.int32

LANES = 128
SUBLANES = 8
VMEM_LIMIT_BYTES = 48 * 1024 * 1024

CHUNK = 64
CHUNK_SHIFT = 6
DSA_QB = 256
EPS = 1e-6
ROPE_THETA = 500000.0
RET_HEADS = 8
RET_DK = 64
RET_THETA = 10000.0
DSA_HEADS = 8
DSA_HEAD_DIM = 64
DSA_ROPE = 16
DSA_LATENT = 128
DSA_VDIM = 64
TOPK_MAX = 256
IDX_HEADS = 8
IDX_DIM = 32
IDX_ROPE = 8
IN_SPLITS = (512, 512, 512, 512, 512, 128, 16, 512, 256, 32, 8)

KPE_LANE = 0
KIDX_LANE = 32
KEY_W = 2 * LANES
CKVT_ROWS = DSA_LATENT + 16

INT_MIN = int(np.iinfo(np.int32).min)
INT_MAX = int(np.iinfo(np.int32).max)
NEG_INF_KEY = INT_MIN + 0x7FFFFF

NT_DIMS = (((1,), (1,)), ((), ()))
LOG2E = math.log2(math.e)


def _dot(a, b):
    return jnp.dot(a, b, preferred_element_type=F32)


def _dot_nt(a, b):
    return lax.dot_general(a, b, NT_DIMS, preferred_element_type=F32)


def _silu(y):
    return y / (1.0 + jnp.exp(-y))


def _in_proj_kernel(x_ref, g_ref, wn_ref, wt_ref, kvg_ref, kvgc_ref,
                    cosr_ref, sinr_ref, coss_ref, sins_ref,
                    cosit_ref, sinit_ref, cosqt_ref, sinqt_ref,
                    rq_ref, rk_ref, rv_ref, rg_ref, dg_ref, keys_ref,
                    ckvt_ref, iqt_ref, dqt_ref, iwt_ref):
    x = x_ref[0]
    tm = x.shape[0]
    ms = jnp.mean(x * x, axis=-1, keepdims=True)
    h = (x * lax.rsqrt(ms + EPS) * g_ref[...]).astype(BF16)

    def nat(c0, width):
        return _dot(h, wn_ref[:, c0:c0 + width])

    cosr = cosr_ref[...]
    sinr = sinr_ref[...]
    lane = lax.broadcasted_iota(I32, (tm, LANES), 1)
    first_half = (lane & (RET_DK // 2)) == 0

    def rope64(y):
        outs = []
        for g in range(y.shape[1] // LANES):
            yg = y[:, LANES * g:LANES * (g + 1)]
            partner = jnp.where(first_half, pltpu.roll(yg, LANES - RET_DK // 2, 1),
                                pltpu.roll(yg, RET_DK // 2, 1))
            outs.append(yg * cosr + partner * sinr)
        return jnp.concatenate(outs, axis=1)

    rw = RET_HEADS * RET_DK
    rq_ref[0] = (rope64(nat(0, rw)) * (RET_DK ** -0.5)).astype(BF16)
    rk_ref[0] = rope64(nat(rw, rw)).astype(BF16)
    rv_ref[0] = nat(2 * rw, rw).astype(BF16)
    rg_ref[0] = _silu(nat(3 * rw, rw)).astype(BF16)
    dg_ref[0] = _silu(nat(4 * rw, rw)).astype(BF16)
    c0 = 5 * rw
    dc = nat(c0, DSA_LATENT)
    ckv = dc * lax.rsqrt(jnp.mean(dc * dc, axis=-1, keepdims=True) + EPS) * kvg_ref[...]
    keys_ref[0, :, 0:LANES] = ckv.astype(BF16)
    small = nat(c0 + LANES, LANES) * coss_ref[...] + nat(c0 + 2 * LANES, LANES) * sins_ref[...]
    keys_ref[0, :, LANES:KEY_W] = small.astype(BF16)

    r = _dot_nt(wt_ref[...], h)
    cit = cosit_ref[...]
    sit = sinit_ref[...]
    pieces = []
    for hh in range(IDX_HEADS):
        base = IDX_DIM * hh
        xr = r[base:base + IDX_ROPE]
        pieces.append(xr * cit + pltpu.roll(xr, IDX_ROPE // 2, 0) * sit)
        pieces.append(r[base + IDX_ROPE:base + IDX_DIM])
    iqt_ref[0] = jnp.concatenate(pieces, axis=0).astype(BF16)

    cqt = cosqt_ref[...]
    sqt = sinqt_ref[...]
    half = DSA_ROPE // 2
    off = IDX_HEADS * IDX_DIM
    pieces = []
    for hh in range(DSA_HEADS):
        base = off + DSA_HEAD_DIM * hh
        x1 = r[base:base + half]
        x2 = r[base + half:base + DSA_ROPE]
        pieces.append(x1 * cqt - x2 * sqt)
        pieces.append(x2 * cqt + x1 * sqt)
        pieces.append(r[base + DSA_ROPE:base + DSA_HEAD_DIM])
    dqt_ref[0] = jnp.concatenate(pieces, axis=0).astype(BF16)

    off += DSA_HEADS * DSA_HEAD_DIM
    dct = r[off:off + DSA_LATENT]
    ckvt = dct * lax.rsqrt(jnp.mean(dct * dct, axis=0, keepdims=True) + EPS) * kvgc_ref[...]
    extra = jnp.where(lax.broadcasted_iota(I32, (CKVT_ROWS - DSA_LATENT, tm), 0) == 0, 1.0, 0.0)
    ckvt_ref[0] = jnp.concatenate([ckvt, extra], axis=0).astype(BF16)
    off += DSA_LATENT
    iwt_ref[0] = r[off:off + IDX_HEADS] * (IDX_HEADS ** -0.5 * IDX_DIM ** -0.5)


def _rope_angles(positions, r, theta):
    inv = theta ** (-jnp.arange(0, r, 2, dtype=F32) / r)
    return positions.astype(F32)[:, None] * inv[None, :]


def _in_proj(x, positions, norm_mix, w_in, kv_norm, tm):
    B, S, D = x.shape
    offs = np.cumsum((0,) + IN_SPLITS)
    w_rq, w_rk, w_rv, w_rg, w_dq, w_dc, w_dpe, w_dg, w_iq, w_ik, w_iw = (
        w_in[:, offs[i]:offs[i + 1]] for i in range(len(IN_SPLITS)))

    hp, hi = DSA_ROPE // 2, IDX_ROPE // 2
    small_a = jnp.zeros((D, LANES), F32)
    small_a = small_a.at[:, KPE_LANE:KPE_LANE + DSA_ROPE].set(w_dpe)
    small_a = small_a.at[:, KIDX_LANE:KIDX_LANE + IDX_DIM].set(w_ik)
    small_b = jnp.zeros((D, LANES), F32)
    small_b = small_b.at[:, KPE_LANE:KPE_LANE + hp].set(-w_dpe[:, hp:])
    small_b = small_b.at[:, KPE_LANE + hp:KPE_LANE + DSA_ROPE].set(w_dpe[:, :hp])
    small_b = small_b.at[:, KIDX_LANE:KIDX_LANE + hi].set(-w_ik[:, hi:IDX_ROPE])
    small_b = small_b.at[:, KIDX_LANE + hi:KIDX_LANE + IDX_ROPE].set(w_ik[:, :hi])
    wn = jnp.concatenate([w_rq, w_rk, w_rv, w_rg, w_dg, w_dc, small_a, small_b], axis=1).astype(BF16)
    wt = jnp.concatenate([w_iq.T, w_dq.T, w_dc.T, w_iw.T, jnp.zeros((SUBLANES, D), F32)], axis=0).astype(BF16)
    n_nat = wn.shape[1]
    n_t = wt.shape[0]

    ang_r = _rope_angles(positions, RET_DK, RET_THETA)
    cos_r = jnp.tile(jnp.cos(ang_r), (1, 4))
    sin_r = jnp.tile(jnp.concatenate([-jnp.sin(ang_r), jnp.sin(ang_r)], axis=1), (1, 2))
    ang_p = _rope_angles(positions, DSA_ROPE, ROPE_THETA)
    ang_i = _rope_angles(positions, IDX_ROPE, ROPE_THETA)
    cos_s = jnp.ones((S, LANES), F32)
    cos_s = cos_s.at[:, KPE_LANE:KPE_LANE + DSA_ROPE].set(jnp.tile(jnp.cos(ang_p), (1, 2)))
    cos_s = cos_s.at[:, KIDX_LANE:KIDX_LANE + IDX_ROPE].set(jnp.tile(jnp.cos(ang_i), (1, 2)))
    sin_s = jnp.zeros((S, LANES), F32)
    sin_s = sin_s.at[:, KPE_LANE:KPE_LANE + DSA_ROPE].set(jnp.tile(jnp.sin(ang_p), (1, 2)))
    sin_s = sin_s.at[:, KIDX_LANE:KIDX_LANE + IDX_ROPE].set(jnp.tile(jnp.sin(ang_i), (1, 2)))
    cos_it = jnp.tile(jnp.cos(ang_i).T, (2, 1))
    sin_it = jnp.concatenate([-jnp.sin(ang_i).T, jnp.sin(ang_i).T], axis=0)
    cos_qt = jnp.cos(ang_p).T
    sin_qt = jnp.sin(ang_p).T

    rw = RET_HEADS * RET_DK
    row = lambda b, s: (b, s, 0)
    col = lambda b, s: (b, 0, s)
    full2 = lambda b, s: (0, 0)
    tok_tab = pl.BlockSpec((tm, LANES), lambda b, s: (s, 0))
    col_tab = pl.BlockSpec((SUBLANES, tm), lambda b, s: (0, s))
    out_shape = (
        jax.ShapeDtypeStruct((B, S, rw), BF16),
        jax.ShapeDtypeStruct((B, S, rw), BF16),
        jax.ShapeDtypeStruct((B, S, rw), BF16),
        jax.ShapeDtypeStruct((B, S, rw), BF16),
        jax.ShapeDtypeStruct((B, S, rw), BF16),
        jax.ShapeDtypeStruct((B, S, KEY_W), BF16),
        jax.ShapeDtypeStruct((B, CKVT_ROWS, S), BF16),
        jax.ShapeDtypeStruct((B, IDX_HEADS * IDX_DIM, S), BF16),
        jax.ShapeDtypeStruct((B, DSA_HEADS * DSA_HEAD_DIM, S), BF16),
        jax.ShapeDtypeStruct((B, IDX_HEADS, S), F32),
    )
    out_specs = (
        pl.BlockSpec((1, tm, rw), row), pl.BlockSpec((1, tm, rw), row), pl.BlockSpec((1, tm, rw), row),
        pl.BlockSpec((1, tm, rw), row), pl.BlockSpec((1, tm, rw), row),
        pl.BlockSpec((1, tm, KEY_W), row),
        pl.BlockSpec((1, CKVT_ROWS, tm), col),
        pl.BlockSpec((1, IDX_HEADS * IDX_DIM, tm), col),
        pl.BlockSpec((1, DSA_HEADS * DSA_HEAD_DIM, tm), col),
        pl.BlockSpec((1, IDX_HEADS, tm), col),
    )
    in_specs = [
        pl.BlockSpec((1, tm, D), row),
        pl.BlockSpec((1, D), full2),
        pl.BlockSpec((D, n_nat), full2),
        pl.BlockSpec((n_t, D), full2),
        pl.BlockSpec((1, DSA_LATENT), full2),
        pl.BlockSpec((DSA_LATENT, 1), full2),
        tok_tab, tok_tab, tok_tab, tok_tab, col_tab, col_tab, col_tab, col_tab,
    ]
    return pl.pallas_call(
        _in_proj_kernel,
        grid=(B, S // tm),
        in_specs=in_specs,
        out_specs=out_specs,
        out_shape=out_shape,
        compiler_params=pltpu.CompilerParams(
            dimension_semantics=("arbitrary", "arbitrary"), vmem_limit_bytes=VMEM_LIMIT_BYTES),
        name="in_proj",
    )(x, norm_mix.reshape(1, D), wn, wt, kv_norm.reshape(1, DSA_LATENT), kv_norm.reshape(DSA_LATENT, 1),
      cos_r, sin_r, cos_s, sin_s, cos_it, sin_it, cos_qt, sin_qt)


def _retention_kernel(q_ref, k_ref, v_ref, g_ref, dmat_ref, xi_ref, zeta_ref, gl_ref, nrm_ref,
                      o_ref, state_ref, *, n_blocks, blk):
    lane = lax.broadcasted_iota(I32, (blk, LANES), 1)
    lo = lane < RET_DK
    r2 = lax.broadcasted_iota(I32, (LANES, LANES), 0) < RET_DK
    c2 = lax.broadcasted_iota(I32, (LANES, LANES), 1) < RET_DK
    same_head = r2 == c2
    npair = state_ref.shape[0]
    state_ref[...] = jnp.zeros(state_ref.shape, F32)
    inv_dv = 1.0 / RET_DK

    def body(i, carry):
        r0 = pl.multiple_of(i * blk, blk)
        for p in range(npair):
            cols = slice(LANES * p, LANES * (p + 1))
            q = q_ref[0, pl.ds(r0, blk), cols]
            k = k_ref[0, pl.ds(r0, blk), cols]
            v = v_ref[0, pl.ds(r0, blk), cols]
            qf = q.astype(F32)
            q0 = jnp.where(lo, qf, 0.0).astype(BF16)
            q1 = jnp.where(lo, 0.0, qf).astype(BF16)
            p0 = (_dot_nt(q0, k) * dmat_ref[p, 0]).astype(BF16)
            p1 = (_dot_nt(q1, k) * dmat_ref[p, 1]).astype(BF16)
            o_intra = jnp.where(lo, _dot(p0, v), _dot(p1, v))
            st = state_ref[p]
            o_cross = _dot((qf * xi_ref[p]).astype(BF16), st.astype(BF16))
            o = o_intra + o_cross
            kzt = (k.astype(F32) * zeta_ref[p]).T.astype(BF16)
            state_ref[p] = st * gl_ref[p] + jnp.where(same_head, _dot(kzt, v), 0.0)

            s_lo = jnp.sum(jnp.where(lo, o, 0.0), axis=-1, keepdims=True)
            s_hi = jnp.sum(jnp.where(lo, 0.0, o), axis=-1, keepdims=True)
            d = o - jnp.where(lo, s_lo, s_hi) * inv_dv
            d2 = d * d
            v_lo = jnp.sum(jnp.where(lo, d2, 0.0), axis=-1, keepdims=True)
            v_hi = jnp.sum(jnp.where(lo, 0.0, d2), axis=-1, keepdims=True)
            var = jnp.where(lo, v_lo, v_hi) * inv_dv
            y = d * lax.rsqrt(var + EPS) * nrm_ref[p] * g_ref[0, pl.ds(r0, blk), cols].astype(F32)
            o_ref[0, pl.ds(r0, blk), cols] = y.astype(BF16)
        return carry

    lax.fori_loop(0, n_blocks, body, 0)


def _retention_tables(blk):
    hs = np.arange(RET_HEADS, dtype=np.float64)
    log_g = np.log1p(-np.exp2(-5.0 - hs))
    t = np.arange(blk, dtype=np.float64)
    dist = np.abs(t[:, None] - t[None, :])
    causal = (t[None, :] // CHUNK) <= (t[:, None] // CHUNK)
    dmat = np.where(causal[None], np.exp(log_g[:, None, None] * dist[None]), 0.0)
    xi = np.exp(log_g[:, None] * (t + 1.0)[None, :])
    zeta = np.exp(log_g[:, None] * (blk - 1.0 - t)[None, :])
    gl = np.exp(log_g * blk)
    npair = RET_HEADS // 2

    def lanes(a):
        a = a.reshape(npair, 2, blk)
        return np.repeat(np.transpose(a, (0, 2, 1)), RET_DK, axis=2)

    gl_l = np.repeat(gl.reshape(npair, 1, 2), RET_DK, axis=2)
    f = lambda a: jnp.asarray(a, F32)
    return f(dmat.reshape(npair, 2, blk, blk)), f(lanes(xi)), f(lanes(zeta)), f(gl_l)


def _retention(rq, rk, rv, rg, ret_norm, blk):
    B, S, W = rq.shape
    npair = W // LANES
    dmat, xi, zeta, gl = _retention_tables(blk)
    seq = pl.BlockSpec((1, S, W), lambda b: (b, 0, 0))
    tab = pl.BlockSpec((npair, blk, LANES), lambda b: (0, 0, 0))
    vec = pl.BlockSpec((npair, 1, LANES), lambda b: (0, 0, 0))
    return pl.pallas_call(
        functools.partial(_retention_kernel, n_blocks=S // blk, blk=blk),
        grid=(B,),
        in_specs=[seq, seq, seq, seq,
                  pl.BlockSpec((npair, 2, blk, blk), lambda b: (0, 0, 0, 0)),
                  tab, tab, vec, vec],
        out_specs=seq,
        out_shape=jax.ShapeDtypeStruct((B, S, W), BF16),
        scratch_shapes=[pltpu.VMEM((npair, LANES, LANES), F32)],
        compiler_params=pltpu.CompilerParams(
            dimension_semantics=("arbitrary",), vmem_limit_bytes=VMEM_LIMIT_BYTES),
        name="retention",
    )(rq, rk, rv, rg, dmat, xi, zeta, gl, ret_norm.reshape(npair, 1, LANES))


def _for_tile_groups(n_tiles, body):
    n4 = n_tiles >> 2

    def loop_body(c, carry):
        body(c * 4, 4)
        return carry

    lax.fori_loop(0, n4, loop_body, 0)

    @pl.when((n_tiles & 2) != 0)
    def _():
        body(n4 * 4, 2)

    @pl.when((n_tiles & 1) != 0)
    def _():
        body(n4 * 4 + (n_tiles & 2), 1)


def _dsa_kernel(keys_ref, ckvt_ref, iqt_ref, dqt_ref, iwt_ref, wukt_ref, wuvt_ref, gate_ref,
                o_ref, qit_ref, qt_ref, sc_ref, lg_ref, acc_ref, thr_ref, rcut_ref, cnt_ref, mx_ref,
                *, n_sel, n_qblk):
    j = pl.program_id(1)
    T = DSA_QB
    HT = DSA_HEADS * T
    n_tiles = j + 1
    scale = DSA_HEAD_DIM ** -0.5 * LOG2E

    @pl.when(j == 0)
    def _():
        qit_ref[...] = jnp.zeros(qit_ref.shape, BF16)
        qt_ref[...] = jnp.zeros(qt_ref.shape, BF16)

    for h in range(DSA_HEADS):
        cols = slice(T * h, T * (h + 1))
        r0 = LANES + KIDX_LANE
        qit_ref[r0:r0 + IDX_DIM, cols] = iqt_ref[0, IDX_DIM * h:IDX_DIM * (h + 1), :]
        dq = dqt_ref[0, DSA_HEAD_DIM * h:DSA_HEAD_DIM * (h + 1), :]
        qlat = _dot(wukt_ref[h], dq) * scale
        qt_ref[0:DSA_LATENT, cols] = qlat.astype(BF16)
        r1 = LANES + KPE_LANE
        qt_ref[r1:r1 + DSA_ROPE, cols] = (dq[0:DSA_ROPE].astype(F32) * scale).astype(BF16)

    def key_tile(kt):
        return keys_ref[0, pl.ds(pl.multiple_of(kt * T, T), T), :]

    w = iwt_ref[0]

    def score_tiles(first, g):
        for t in range(g):
            kt = first + t
            s = _dot(key_tile(kt), qit_ref[...])
            score = jnp.zeros((T, T), F32)
            for h in range(IDX_HEADS):
                score = score + jnp.maximum(s[:, T * h:T * (h + 1)], 0.0) * w[h:h + 1, :]
            sc_ref[pl.ds(pl.multiple_of(kt * T, T), T), :] = score

    _for_tile_groups(n_tiles, score_tiles)
    r_hid = pl.multiple_of(j * T + CHUNK, CHUNK)
    key_chunk = (lax.broadcasted_iota(I32, (T - CHUNK, T), 0) >> CHUNK_SHIFT) + 1
    query_chunk = lax.broadcasted_iota(I32, (T - CHUNK, T), 1) >> CHUNK_SHIFT
    sc_ref[pl.ds(r_hid, T - CHUNK), :] = jnp.where(key_chunk > query_chunk, -jnp.inf,
                                                   sc_ref[pl.ds(r_hid, T - CHUNK), :])
    sc_ref[pl.ds(pl.multiple_of(n_tiles * T, T), T), :] = jnp.full((T, T), -jnp.inf, F32)

    trows = T // SUBLANES
    full = lambda v: jnp.full((SUBLANES, T), v, I32)
    sub_total = lambda acc: jnp.broadcast_to(jnp.sum(acc, axis=0, keepdims=True), (SUBLANES, T))
    key_to_float = lambda key: lax.bitcast_convert_type(key ^ ((key >> 31) & INT_MAX), F32)
    thr_ref[...] = jnp.full((SUBLANES, T), -jnp.inf, F32)
    rcut_ref[...] = full(0)
    cnt_ref[...] = full(n_sel)

    def descend(nt):
        def bit_step(i, c):
            thr, cnt_thr = c
            cand = thr + lax.shift_left(jnp.int32(1), 31 - i)
            cand_f = key_to_float(cand)
            accs = [full(0)] * 4
            for idx in range(nt * trows):
                row = SUBLANES * idx
                accs[idx % 4] = accs[idx % 4] + jnp.where(sc_ref[row:row + SUBLANES, :] >= cand_f, 1, 0)
            cnt = sub_total((accs[0] + accs[1]) + (accs[2] + accs[3]))
            take = cnt >= n_sel
            return jnp.where(take, cand, thr), jnp.where(take, cnt, cnt_thr)

        thr, cnt_thr = lax.fori_loop(0, 32, bit_step, (full(INT_MIN), full(nt * T)))
        few = thr <= NEG_INF_KEY
        thr_ref[...] = jnp.where(few, -jnp.inf, key_to_float(thr))
        cnt_ref[...] = jnp.where(few, n_sel, cnt_thr)
        rcut_ref[...] = jnp.where(few, 0, INT_MAX)

    for nt in range(n_sel // T + 1, n_qblk + 1):
        pl.when(n_tiles == nt)(functools.partial(descend, nt))

    @pl.when(jnp.max(jnp.where(cnt_ref[...] > n_sel, 1, 0)) > 0)
    def _break_ties():
        C2 = 2 * T
        n_chunks = (n_tiles + 1) // 2
        vrows = C2 // SUBLANES
        row_bits = int(n_qblk * T + T).bit_length()
        thr = thr_ref[...]

        def count(pred):
            def body(c, acc):
                r0 = pl.multiple_of(c * C2, C2)
                sc = sc_ref[pl.ds(r0, C2), :].reshape(vrows, SUBLANES, T)
                rid = (lax.broadcasted_iota(I32, (vrows, SUBLANES, T), 0) * SUBLANES
                       + lax.broadcasted_iota(I32, (vrows, SUBLANES, T), 1) + c * C2)
                return acc + jnp.sum(pred(sc, rid), axis=0)
            return sub_total(lax.fori_loop(0, n_chunks, body, full(0)))

        need = n_sel - count(lambda sc, rid: jnp.where(sc > thr[None], 1, 0))

        def row_step(i, rc):
            cand = rc + lax.shift_left(jnp.int32(1), row_bits - 1 - i)
            cnt = count(lambda sc, rid: jnp.where(sc == thr[None], jnp.where(rid < cand[None], 1, 0), 0))
            return jnp.where(cnt <= need, cand, rc)
        rcut_ref[...] = jnp.where(cnt_ref[...] > n_sel, lax.fori_loop(0, row_bits, row_step, full(0)), rcut_ref[...])

    thr_f = thr_ref[...]
    rcut = rcut_ref[...]

    mx_ref[...] = jnp.full((SUBLANES, HT), -jnp.inf, F32)

    def logit_tiles(first, g):
        mx = mx_ref[...]
        for t in range(g):
            kt = first + t
            r0 = pl.multiple_of(kt * T, T)
            l = _dot(key_tile(kt), qt_ref[...])
            sc = sc_ref[pl.ds(r0, T), :].reshape(trows, SUBLANES, T)
            rid = (lax.broadcasted_iota(I32, (trows, SUBLANES, T), 0) * SUBLANES
                   + lax.broadcasted_iota(I32, (trows, SUBLANES, T), 1) + kt * T)
            tie = jnp.where(sc == thr_f[None], jnp.where(rid < rcut[None], 0.0, -jnp.inf), -jnp.inf)
            bias = jnp.where(sc > thr_f[None], 0.0, tie).reshape(T, T)
            new = []
            for h in range(DSA_HEADS):
                lm = l[:, T * h:T * (h + 1)] + bias
                lg_ref[pl.ds(r0, T), T * h:T * (h + 1)] = lm
                new.append(jnp.max(lm.reshape(trows, SUBLANES, T), axis=0))
            mx = jnp.maximum(mx, jnp.concatenate(new, axis=1))
        mx_ref[...] = mx

    _for_tile_groups(n_tiles, logit_tiles)
    mrow = jnp.broadcast_to(jnp.max(mx_ref[...], axis=0, keepdims=True), (SUBLANES, HT))

    acc_ref[...] = jnp.zeros(acc_ref.shape, F32)

    def pv_tiles(first, g):
        r0 = pl.multiple_of(first * T, T)
        ps = []
        for t in range(g):
            lm = lg_ref[pl.ds(pl.multiple_of((first + t) * T, T), T), :]
            p = jnp.exp2(lm.reshape(trows, SUBLANES, HT) - mrow[None])
            ps.append(p.reshape(T, HT).astype(BF16))
        p_all = ps[0] if g == 1 else jnp.concatenate(ps, axis=0)
        acc_ref[...] += _dot(ckvt_ref[0, :, pl.ds(r0, g * T)], p_all)

    _for_tile_groups(n_tiles, pv_tiles)
    linv = 1.0 / acc_ref[DSA_LATENT:DSA_LATENT + 1, :]
    olat = (acc_ref[0:DSA_LATENT, :] * linv).astype(BF16)
    outs = [_dot(wuvt_ref[h], olat[:, T * h:T * (h + 1)]) for h in range(DSA_HEADS)]
    o = jnp.concatenate(outs, axis=0).T
    o_ref[0] = (o * gate_ref[0].astype(F32)).astype(BF16)


def _dsa(keys, ckvt, iqt, dqt, iwt, w_uk, w_uv, dgate):
    B, S, _ = keys.shape
    T = DSA_QB
    assert S % T == 0 and T % CHUNK == 0
    n_qblk = S // T
    n_sel = min(TOPK_MAX, S // 4)
    H = DSA_HEADS
    wukt = jnp.concatenate([jnp.zeros((H, DSA_LATENT, DSA_ROPE), F32), jnp.swapaxes(w_uk, 1, 2)], axis=2).astype(BF16)
    wuvt = jnp.swapaxes(w_uv, 1, 2).astype(BF16)
    HT = H * T
    per_b = lambda b, j: (b, 0, 0)
    qcol = lambda b, j: (b, 0, j)
    full3 = lambda b, j: (0, 0, 0)
    return pl.pallas_call(
        functools.partial(_dsa_kernel, n_sel=n_sel, n_qblk=n_qblk),
        grid=(B, n_qblk),
        in_specs=[
            pl.BlockSpec((1, S, KEY_W), per_b),
            pl.BlockSpec((1, CKVT_ROWS, S), per_b),
            pl.BlockSpec((1, IDX_HEADS * IDX_DIM, T), qcol),
            pl.BlockSpec((1, H * DSA_HEAD_DIM, T), qcol),
            pl.BlockSpec((1, IDX_HEADS, T), qcol),
            pl.BlockSpec((H, DSA_LATENT, DSA_HEAD_DIM), full3),
            pl.BlockSpec((H, DSA_VDIM, DSA_LATENT), full3),
            pl.BlockSpec((1, T, H * DSA_VDIM), lambda b, j: (b, j, 0)),
        ],
        out_specs=pl.BlockSpec((1, T, H * DSA_VDIM), lambda b, j: (b, j, 0)),
        out_shape=jax.ShapeDtypeStruct((B, S, H * DSA_VDIM), BF16),
        scratch_shapes=[
            pltpu.VMEM((KEY_W, HT), BF16),
            pltpu.VMEM((KEY_W, HT), BF16),
            pltpu.VMEM((S + T, T), F32),
            pltpu.VMEM((S, HT), F32),
            pltpu.VMEM((CKVT_ROWS, HT), F32),
            pltpu.VMEM((SUBLANES, T), F32),
            pltpu.VMEM((SUBLANES, T), I32),
            pltpu.VMEM((SUBLANES, T), I32),
            pltpu.VMEM((SUBLANES, HT), F32),
        ],
        compiler_params=pltpu.CompilerParams(
            dimension_semantics=("arbitrary", "arbitrary"), vmem_limit_bytes=VMEM_LIMIT_BYTES),
        name="dsa",
    )(keys, ckvt, iqt, dqt, iwt, wukt, wuvt, dgate)


def _out_kernel(x_ref, ret_ref, dsa_ref, p_ref, wo_ref, gp_ref, wg_ref, wp_ref, gf_ref, o_ref, *, final_norm):
    nr = ret_ref.shape[2]
    x1 = x_ref[0] + _dot(ret_ref[0], wo_ref[0:nr, :]) + _dot(dsa_ref[0], wo_ref[nr:, :])
    ms = jnp.mean(x1 * x1, axis=-1, keepdims=True)
    h = (x1 * lax.rsqrt(ms + EPS) * gp_ref[...]).astype(BF16)
    z = _dot(h, wg_ref[...])
    gate = 1.0 / (1.0 + jnp.exp(-z))
    x2 = x1 + gate * _dot(p_ref[0].astype(BF16), wp_ref[...])
    if final_norm:
        ms2 = jnp.mean(x2 * x2, axis=-1, keepdims=True)
        x2 = x2 * lax.rsqrt(ms2 + EPS) * gf_ref[...]
    o_ref[0] = x2


def _out_proj(x, ret, dsa, p, w_out, norm_ple, w_gate, w_proj, norm_final, tm, final_norm):
    B, S, D = x.shape
    row = lambda b, s: (b, s, 0)
    full2 = lambda b, s: (0, 0)
    return pl.pallas_call(
        functools.partial(_out_kernel, final_norm=final_norm),
        grid=(B, S // tm),
        in_specs=[
            pl.BlockSpec((1, tm, D), row),
            pl.BlockSpec((1, tm, ret.shape[2]), row),
            pl.BlockSpec((1, tm, dsa.shape[2]), row),
            pl.BlockSpec((1, tm, p.shape[2]), row),
            pl.BlockSpec(w_out.shape, full2),
            pl.BlockSpec((1, D), full2),
            pl.BlockSpec(w_gate.shape, full2),
            pl.BlockSpec(w_proj.shape, full2),
            pl.BlockSpec((1, D), full2),
        ],
        out_specs=pl.BlockSpec((1, tm, D), row),
        out_shape=jax.ShapeDtypeStruct((B, S, D), F32),
        compiler_params=pltpu.CompilerParams(
            dimension_semantics=("arbitrary", "arbitrary"), vmem_limit_bytes=VMEM_LIMIT_BYTES),
        name="out_proj",
    )(x, ret, dsa, p, w_out.astype(BF16), norm_ple.reshape(1, D), w_gate.astype(BF16),
      w_proj.astype(BF16), norm_final.reshape(1, D))


def _tiles(S):
    tm = math.gcd(S, 512)
    blk = math.gcd(S, 256)
    return tm, blk


def kernel(x, p, positions, norm_mix, w_in, ret_norm, kv_norm, w_uk, w_uv, w_out, norm_ple, w_ple_gate, w_ple_proj, norm_final):
    B, S, D = x.shape
    depth = w_in.shape[0]
    tm, blk = _tiles(S)
    assert blk % CHUNK == 0 and (1 << CHUNK_SHIFT) == CHUNK
    for i in range(depth):
        rq, rk, rv, rg, dg, keys, ckvt, iqt, dqt, iwt = _in_proj(x, positions, norm_mix[i], w_in[i], kv_norm[i], tm)
        ret = _retention(rq, rk, rv, rg, ret_norm[i], blk)
        dsa = _dsa(keys, ckvt, iqt, dqt, iwt, w_uk[i], w_uv[i], dg)
        x = _out_proj(x, ret, dsa, p[i], w_out[i], norm_ple[i], w_ple_gate[i], w_ple_proj[i],
                      norm_final, tm, final_norm=(i == depth - 1))
    return x
```

```python
import functools
import math

import numpy as np
import jax
import jax.numpy as jnp
from jax import lax
from jax.experimental import pallas as pl
from jax.experimental.pallas import tpu as pltpu

F32 = jnp.float32
BF16 = jnp.bfloat16
I32 = jnp.int32

LANES = 128
SUBLANES = 8
VMEM_LIMIT_BYTES = 48 * 1024 * 1024

CHUNK = 64
CHUNK_SHIFT = 6
DSA_QB = 256
EPS = 1e-6
ROPE_THETA = 500000.0
RET_HEADS = 8
RET_DK = 64
RET_THETA = 10000.0
DSA_HEADS = 8
DSA_HEAD_DIM = 64
DSA_ROPE = 16
DSA_LATENT = 128
DSA_VDIM = 64
TOPK_MAX = 256
IDX_HEADS = 8
IDX_DIM = 32
IDX_ROPE = 8
IN_SPLITS = (512, 512, 512, 512, 512, 128, 16, 512, 256, 32, 8)

KPE_LANE = 0
KIDX_LANE = 32
KEY_W = 2 * LANES
CKVT_ROWS = DSA_LATENT + 16

INT_MIN = int(np.iinfo(np.int32).min)
INT_MAX = int(np.iinfo(np.int32).max)
NEG_INF_KEY = INT_MIN + 0x7FFFFF
F32_LOWEST = float(np.finfo(np.float32).min)

NT_DIMS = (((1,), (1,)), ((), ()))
LOG2E = math.log2(math.e)


def _dot(a, b):
    return jnp.dot(a, b, preferred_element_type=F32)


def _dot_nt(a, b):
    return lax.dot_general(a, b, NT_DIMS, preferred_element_type=F32)


def _silu(y):
    return y / (1.0 + jnp.exp(-y))


def _in_proj_kernel(x_ref, g_ref, wn_ref, wt_ref, kvg_ref, kvgc_ref,
                    cosr_ref, sinr_ref, coss_ref, sins_ref,
                    cosit_ref, sinit_ref, cosqt_ref, sinqt_ref,
                    rq_ref, rk_ref, rv_ref, rg_ref, dg_ref, keys_ref,
                    ckvt_ref, iqt_ref, dqt_ref, iwt_ref):
    x = x_ref[0]
    tm = x.shape[0]
    ms = jnp.mean(x * x, axis=-1, keepdims=True)
    h = (x * lax.rsqrt(ms + EPS) * g_ref[...]).astype(BF16)

    def nat(c0, width):
        return _dot(h, wn_ref[:, c0:c0 + width])

    cosr = cosr_ref[...]
    sinr = sinr_ref[...]
    lane = lax.broadcasted_iota(I32, (tm, LANES), 1)
    first_half = (lane & (RET_DK // 2)) == 0

    def rope64(y):
        outs = []
        for g in range(y.shape[1] // LANES):
            yg = y[:, LANES * g:LANES * (g + 1)]
            partner = jnp.where(first_half, pltpu.roll(yg, LANES - RET_DK // 2, 1),
                                pltpu.roll(yg, RET_DK // 2, 1))
            outs.append(yg * cosr + partner * sinr)
        return jnp.concatenate(outs, axis=1)

    rw = RET_HEADS * RET_DK
    rq_ref[0] = (rope64(nat(0, rw)) * (RET_DK ** -0.5)).astype(BF16)
    rk_ref[0] = rope64(nat(rw, rw)).astype(BF16)
    rv_ref[0] = nat(2 * rw, rw).astype(BF16)
    rg_ref[0] = _silu(nat(3 * rw, rw)).astype(BF16)
    dg_ref[0] = _silu(nat(4 * rw, rw)).astype(BF16)
    c0 = 5 * rw
    dc = nat(c0, DSA_LATENT)
    ckv = dc * lax.rsqrt(jnp.mean(dc * dc, axis=-1, keepdims=True) + EPS) * kvg_ref[...]
    keys_ref[0, :, 0:LANES] = ckv.astype(BF16)
    small = nat(c0 + LANES, LANES) * coss_ref[...] + nat(c0 + 2 * LANES, LANES) * sins_ref[...]
    keys_ref[0, :, LANES:KEY_W] = small.astype(BF16)

    r = _dot_nt(wt_ref[...], h)
    cit = cosit_ref[...]
    sit = sinit_ref[...]
    pieces = []
    for hh in range(IDX_HEADS):
        base = IDX_DIM * hh
        xr = r[base:base + IDX_ROPE]
        pieces.append(xr * cit + pltpu.roll(xr, IDX_ROPE // 2, 0) * sit)
        pieces.append(r[base + IDX_ROPE:base + IDX_DIM])
    iqt_ref[0] = jnp.concatenate(pieces, axis=0).astype(BF16)

    cqt = cosqt_ref[...]
    sqt = sinqt_ref[...]
    half = DSA_ROPE // 2
    off = IDX_HEADS * IDX_DIM
    pieces = []
    for hh in range(DSA_HEADS):
        base = off + DSA_HEAD_DIM * hh
        x1 = r[base:base + half]
        x2 = r[base + half:base + DSA_ROPE]
        pieces.append(x1 * cqt - x2 * sqt)
        pieces.append(x2 * cqt + x1 * sqt)
        pieces.append(r[base + DSA_ROPE:base + DSA_HEAD_DIM])
    dqt_ref[0] = jnp.concatenate(pieces, axis=0).astype(BF16)

    off += DSA_HEADS * DSA_HEAD_DIM
    dct = r[off:off + DSA_LATENT]
    ckvt = dct * lax.rsqrt(jnp.mean(dct * dct, axis=0, keepdims=True) + EPS) * kvgc_ref[...]
    extra = jnp.where(lax.broadcasted_iota(I32, (CKVT_ROWS - DSA_LATENT, tm), 0) == 0, 1.0, 0.0)
    ckvt_ref[0] = jnp.concatenate([ckvt, extra], axis=0).astype(BF16)
    off += DSA_LATENT
    iwt_ref[0] = r[off:off + IDX_HEADS] * (IDX_HEADS ** -0.5 * IDX_DIM ** -0.5)


def _rope_angles(positions, r, theta):
    inv = theta ** (-jnp.arange(0, r, 2, dtype=F32) / r)
    return positions.astype(F32)[:, None] * inv[None, :]


def _in_proj(x, positions, norm_mix, w_in, kv_norm, tm):
    B, S, D = x.shape
    offs = np.cumsum((0,) + IN_SPLITS)
    w_rq, w_rk, w_rv, w_rg, w_dq, w_dc, w_dpe, w_dg, w_iq, w_ik, w_iw = (
        w_in[:, offs[i]:offs[i + 1]] for i in range(len(IN_SPLITS)))

    hp, hi = DSA_ROPE // 2, IDX_ROPE // 2
    small_a = jnp.zeros((D, LANES), F32)
    small_a = small_a.at[:, KPE_LANE:KPE_LANE + DSA_ROPE].set(w_dpe)
    small_a = small_a.at[:, KIDX_LANE:KIDX_LANE + IDX_DIM].set(w_ik)
    small_b = jnp.zeros((D, LANES), F32)
    small_b = small_b.at[:, KPE_LANE:KPE_LANE + hp].set(-w_dpe[:, hp:])
    small_b = small_b.at[:, KPE_LANE + hp:KPE_LANE + DSA_ROPE].set(w_dpe[:, :hp])
    small_b = small_b.at[:, KIDX_LANE:KIDX_LANE + hi].set(-w_ik[:, hi:IDX_ROPE])
    small_b = small_b.at[:, KIDX_LANE + hi:KIDX_LANE + IDX_ROPE].set(w_ik[:, :hi])
    wn = jnp.concatenate([w_rq, w_rk, w_rv, w_rg, w_dg, w_dc, small_a, small_b], axis=1).astype(BF16)
    wt = jnp.concatenate([w_iq.T, w_dq.T, w_dc.T, w_iw.T, jnp.zeros((SUBLANES, D), F32)], axis=0).astype(BF16)
    n_nat = wn.shape[1]
    n_t = wt.shape[0]

    ang_r = _rope_angles(positions, RET_DK, RET_THETA)
    cos_r = jnp.tile(jnp.cos(ang_r), (1, 4))
    sin_r = jnp.tile(jnp.concatenate([-jnp.sin(ang_r), jnp.sin(ang_r)], axis=1), (1, 2))
    ang_p = _rope_angles(positions, DSA_ROPE, ROPE_THETA)
    ang_i = _rope_angles(positions, IDX_ROPE, ROPE_THETA)
    cos_s = jnp.ones((S, LANES), F32)
    cos_s = cos_s.at[:, KPE_LANE:KPE_LANE + DSA_ROPE].set(jnp.tile(jnp.cos(ang_p), (1, 2)))
    cos_s = cos_s.at[:, KIDX_LANE:KIDX_LANE + IDX_ROPE].set(jnp.tile(jnp.cos(ang_i), (1, 2)))
    sin_s = jnp.zeros((S, LANES), F32)
    sin_s = sin_s.at[:, KPE_LANE:KPE_LANE + DSA_ROPE].set(jnp.tile(jnp.sin(ang_p), (1, 2)))
    sin_s = sin_s.at[:, KIDX_LANE:KIDX_LANE + IDX_ROPE].set(jnp.tile(jnp.sin(ang_i), (1, 2)))
    cos_it = jnp.tile(jnp.cos(ang_i).T, (2, 1))
    sin_it = jnp.concatenate([-jnp.sin(ang_i).T, jnp.sin(ang_i).T], axis=0)
    cos_qt = jnp.cos(ang_p).T
    sin_qt = jnp.sin(ang_p).T

    rw = RET_HEADS * RET_DK
    row = lambda b, s: (b, s, 0)
    col = lambda b, s: (b, 0, s)
    full2 = lambda b, s: (0, 0)
    tok_tab = pl.BlockSpec((tm, LANES), lambda b, s: (s, 0))
    col_tab = pl.BlockSpec((SUBLANES, tm), lambda b, s: (0, s))
    out_shape = (
        jax.ShapeDtypeStruct((B, S, rw), BF16),
        jax.ShapeDtypeStruct((B, S, rw), BF16),
        jax.ShapeDtypeStruct((B, S, rw), BF16),
        jax.ShapeDtypeStruct((B, S, rw), BF16),
        jax.ShapeDtypeStruct((B, S, rw), BF16),
        jax.ShapeDtypeStruct((B, S, KEY_W), BF16),
        jax.ShapeDtypeStruct((B, CKVT_ROWS, S), BF16),
        jax.ShapeDtypeStruct((B, IDX_HEADS * IDX_DIM, S), BF16),
        jax.ShapeDtypeStruct((B, DSA_HEADS * DSA_HEAD_DIM, S), BF16),
        jax.ShapeDtypeStruct((B, IDX_HEADS, S), F32),
    )
    out_specs = (
        pl.BlockSpec((1, tm, rw), row), pl.BlockSpec((1, tm, rw), row), pl.BlockSpec((1, tm, rw), row),
        pl.BlockSpec((1, tm, rw), row), pl.BlockSpec((1, tm, rw), row),
        pl.BlockSpec((1, tm, KEY_W), row),
        pl.BlockSpec((1, CKVT_ROWS, tm), col),
        pl.BlockSpec((1, IDX_HEADS * IDX_DIM, tm), col),
        pl.BlockSpec((1, DSA_HEADS * DSA_HEAD_DIM, tm), col),
        pl.BlockSpec((1, IDX_HEADS, tm), col),
    )
    in_specs = [
        pl.BlockSpec((1, tm, D), row),
        pl.BlockSpec((1, D), full2),
        pl.BlockSpec((D, n_nat), full2),
        pl.BlockSpec((n_t, D), full2),
        pl.BlockSpec((1, DSA_LATENT), full2),
        pl.BlockSpec((DSA_LATENT, 1), full2),
        tok_tab, tok_tab, tok_tab, tok_tab, col_tab, col_tab, col_tab, col_tab,
    ]
    return pl.pallas_call(
        _in_proj_kernel,
        grid=(B, S // tm),
        in_specs=in_specs,
        out_specs=out_specs,
        out_shape=out_shape,
        compiler_params=pltpu.CompilerParams(
            dimension_semantics=("arbitrary", "arbitrary"), vmem_limit_bytes=VMEM_LIMIT_BYTES),
        name="in_proj",
    )(x, norm_mix.reshape(1, D), wn, wt, kv_norm.reshape(1, DSA_LATENT), kv_norm.reshape(DSA_LATENT, 1),
      cos_r, sin_r, cos_s, sin_s, cos_it, sin_it, cos_qt, sin_qt)


def _retention_kernel(q_ref, k_ref, v_ref, g_ref, dmat_ref, xi_ref, zeta_ref, gl_ref, nrm_ref,
                      o_ref, state_ref, *, n_blocks, blk):
    lane = lax.broadcasted_iota(I32, (blk, LANES), 1)
    lo = lane < RET_DK
    r2 = lax.broadcasted_iota(I32, (LANES, LANES), 0) < RET_DK
    c2 = lax.broadcasted_iota(I32, (LANES, LANES), 1) < RET_DK
    same_head = r2 == c2
    npair = state_ref.shape[0]
    state_ref[...] = jnp.zeros(state_ref.shape, F32)
    inv_dv = 1.0 / RET_DK

    def body(i, carry):
        r0 = pl.multiple_of(i * blk, blk)
        for p in range(npair):
            cols = slice(LANES * p, LANES * (p + 1))
            q = q_ref[0, pl.ds(r0, blk), cols]
            k = k_ref[0, pl.ds(r0, blk), cols]
            v = v_ref[0, pl.ds(r0, blk), cols]
            qf = q.astype(F32)
            q0 = jnp.where(lo, qf, 0.0).astype(BF16)
            q1 = jnp.where(lo, 0.0, qf).astype(BF16)
            p0 = (_dot_nt(q0, k) * dmat_ref[p, 0]).astype(BF16)
            p1 = (_dot_nt(q1, k) * dmat_ref[p, 1]).astype(BF16)
            o_intra = jnp.where(lo, _dot(p0, v), _dot(p1, v))
            st = state_ref[p]
            o_cross = _dot((qf * xi_ref[p]).astype(BF16), st.astype(BF16))
            o = o_intra + o_cross
            kzt = (k.astype(F32) * zeta_ref[p]).T.astype(BF16)
            state_ref[p] = st * gl_ref[p] + jnp.where(same_head, _dot(kzt, v), 0.0)

            s_lo = jnp.sum(jnp.where(lo, o, 0.0), axis=-1, keepdims=True)
            s_hi = jnp.sum(jnp.where(lo, 0.0, o), axis=-1, keepdims=True)
            d = o - jnp.where(lo, s_lo, s_hi) * inv_dv
            d2 = d * d
            v_lo = jnp.sum(jnp.where(lo, d2, 0.0), axis=-1, keepdims=True)
            v_hi = jnp.sum(jnp.where(lo, 0.0, d2), axis=-1, keepdims=True)
            var = jnp.where(lo, v_lo, v_hi) * inv_dv
            y = d * lax.rsqrt(var + EPS) * nrm_ref[p] * g_ref[0, pl.ds(r0, blk), cols].astype(F32)
            o_ref[0, pl.ds(r0, blk), cols] = y.astype(BF16)
        return carry

    lax.fori_loop(0, n_blocks, body, 0)


def _retention_tables(blk):
    hs = np.arange(RET_HEADS, dtype=np.float64)
    log_g = np.log1p(-np.exp2(-5.0 - hs))
    t = np.arange(blk, dtype=np.float64)
    dist = np.abs(t[:, None] - t[None, :])
    causal = (t[None, :] // CHUNK) <= (t[:, None] // CHUNK)
    dmat = np.where(causal[None], np.exp(log_g[:, None, None] * dist[None]), 0.0)
    xi = np.exp(log_g[:, None] * (t + 1.0)[None, :])
    zeta = np.exp(log_g[:, None] * (blk - 1.0 - t)[None, :])
    gl = np.exp(log_g * blk)
    npair = RET_HEADS // 2

    def lanes(a):
        a = a.reshape(npair, 2, blk)
        return np.repeat(np.transpose(a, (0, 2, 1)), RET_DK, axis=2)

    gl_l = np.repeat(gl.reshape(npair, 1, 2), RET_DK, axis=2)
    f = lambda a: jnp.asarray(a, F32)
    return f(dmat.reshape(npair, 2, blk, blk)), f(lanes(xi)), f(lanes(zeta)), f(gl_l)


def _retention(rq, rk, rv, rg, ret_norm, blk):
    B, S, W = rq.shape
    npair = W // LANES
    dmat, xi, zeta, gl = _retention_tables(blk)
    seq = pl.BlockSpec((1, S, W), lambda b: (b, 0, 0))
    tab = pl.BlockSpec((npair, blk, LANES), lambda b: (0, 0, 0))
    vec = pl.BlockSpec((npair, 1, LANES), lambda b: (0, 0, 0))
    return pl.pallas_call(
        functools.partial(_retention_kernel, n_blocks=S // blk, blk=blk),
        grid=(B,),
        in_specs=[seq, seq, seq, seq,
                  pl.BlockSpec((npair, 2, blk, blk), lambda b: (0, 0, 0, 0)),
                  tab, tab, vec, vec],
        out_specs=seq,
        out_shape=jax.ShapeDtypeStruct((B, S, W), BF16),
        scratch_shapes=[pltpu.VMEM((npair, LANES, LANES), F32)],
        compiler_params=pltpu.CompilerParams(
            dimension_semantics=("arbitrary",), vmem_limit_bytes=VMEM_LIMIT_BYTES),
        name="retention",
    )(rq, rk, rv, rg, dmat, xi, zeta, gl, ret_norm.reshape(npair, 1, LANES))


def _for_tile_groups(n_tiles, body):
    n4 = n_tiles >> 2

    def loop_body(c, carry):
        body(c * 4, 4)
        return carry

    lax.fori_loop(0, n4, loop_body, 0)

    @pl.when((n_tiles & 2) != 0)
    def _():
        body(n4 * 4, 2)

    @pl.when((n_tiles & 1) != 0)
    def _():
        body(n4 * 4 + (n_tiles & 2), 1)


def _dsa_kernel(keys_ref, ckvt_ref, iqt_ref, dqt_ref, iwt_ref, wukt_ref, wuvt_ref, gate_ref,
                o_ref, qit_ref, qt_ref, sc_ref, lg_ref, acc_ref, thr_ref, cnt_ref, mx_ref,
                *, n_sel, n_qblk):
    j = pl.program_id(1)
    T = DSA_QB
    HT = DSA_HEADS * T
    n_tiles = j + 1
    scale = DSA_HEAD_DIM ** -0.5 * LOG2E

    @pl.when(j == 0)
    def _():
        qit_ref[...] = jnp.zeros(qit_ref.shape, BF16)
        qt_ref[...] = jnp.zeros(qt_ref.shape, BF16)

    for h in range(DSA_HEADS):
        cols = slice(T * h, T * (h + 1))
        r0 = LANES + KIDX_LANE
        qit_ref[r0:r0 + IDX_DIM, cols] = iqt_ref[0, IDX_DIM * h:IDX_DIM * (h + 1), :]
        dq = dqt_ref[0, DSA_HEAD_DIM * h:DSA_HEAD_DIM * (h + 1), :]
        qlat = _dot(wukt_ref[h], dq) * scale
        qt_ref[0:DSA_LATENT, cols] = qlat.astype(BF16)
        r1 = LANES + KPE_LANE
        qt_ref[r1:r1 + DSA_ROPE, cols] = (dq[0:DSA_ROPE].astype(F32) * scale).astype(BF16)

    def key_tile(kt):
        return keys_ref[0, pl.ds(pl.multiple_of(kt * T, T), T), :]

    w = iwt_ref[0]

    def score_tiles(first, g):
        for t in range(g):
            kt = first + t
            s = _dot(key_tile(kt), qit_ref[...])
            score = jnp.zeros((T, T), F32)
            for h in range(IDX_HEADS):
                score = score + jnp.maximum(s[:, T * h:T * (h + 1)], 0.0) * w[h:h + 1, :]
            sc_ref[pl.ds(pl.multiple_of(kt * T, T), T), :] = score

    _for_tile_groups(n_tiles, score_tiles)
    r_hid = pl.multiple_of(j * T + CHUNK, CHUNK)
    key_chunk = (lax.broadcasted_iota(I32, (T - CHUNK, T), 0) >> CHUNK_SHIFT) + 1
    query_chunk = lax.broadcasted_iota(I32, (T - CHUNK, T), 1) >> CHUNK_SHIFT
    sc_ref[pl.ds(r_hid, T - CHUNK), :] = jnp.where(key_chunk > query_chunk, -jnp.inf,
                                                   sc_ref[pl.ds(r_hid, T - CHUNK), :])

    trows = T // SUBLANES
    full = lambda v: jnp.full((SUBLANES, T), v, I32)
    sub_total = lambda acc: jnp.broadcast_to(jnp.sum(acc, axis=0, keepdims=True), (SUBLANES, T))
    key_to_float = lambda key: lax.bitcast_convert_type(key ^ ((key >> 31) & INT_MAX), F32)
    thr_ref[...] = jnp.full((SUBLANES, T), F32_LOWEST, F32)
    cnt_ref[...] = full(n_sel)

    def descend(nt):
        def bit_step(i, c):
            thr, cnt_thr = c
            cand = thr + lax.shift_left(jnp.int32(1), 31 - i)
            cand_f = key_to_float(cand)
            accs = [full(0)] * 4
            for idx in range(nt * trows):
                row = SUBLANES * idx
                accs[idx % 4] = accs[idx % 4] + jnp.where(sc_ref[row:row + SUBLANES, :] >= cand_f, 1, 0)
            cnt = sub_total((accs[0] + accs[1]) + (accs[2] + accs[3]))
            take = cnt >= n_sel
            return jnp.where(take, cand, thr), jnp.where(take, cnt, cnt_thr)

        thr, cnt_thr = lax.fori_loop(0, 32, bit_step, (full(INT_MIN), full(nt * T)))
        few = thr <= NEG_INF_KEY
        thr_ref[...] = jnp.where(few, F32_LOWEST, key_to_float(thr))
        cnt_ref[...] = jnp.where(few, n_sel, cnt_thr)

    for nt in range(n_sel // T + 1, n_qblk + 1):
        pl.when(n_tiles == nt)(functools.partial(descend, nt))

    @pl.when(jnp.max(jnp.where(cnt_ref[...] > n_sel, 1, 0)) > 0)
    def _demote_excess_ties():
        thr = thr_ref[...]
        excess = cnt_ref[...] - n_sel
        sub = lax.broadcasted_iota(I32, (SUBLANES, T), 0)

        def tile_body(i, later):
            rows = pl.ds(pl.multiple_of((n_tiles - 1 - i) * T, T), T)
            tile = sc_ref[rows, :]
            kept = [None] * trows
            for v in range(trows - 1, -1, -1):
                sc = tile[SUBLANES * v:SUBLANES * (v + 1), :]
                tied = jnp.where(sc == thr, 1, 0)
                after = tied
                for k in (1, 2, 4):
                    after = after + jnp.where(sub < SUBLANES - k, pltpu.roll(after, SUBLANES - k, 0), 0)
                rank_from_end = later + after - tied
                drop = jnp.where(tied > 0, jnp.where(rank_from_end < excess, 1, 0), 0)
                kept[v] = jnp.where(drop > 0, -jnp.inf, sc)
                later = later + jnp.broadcast_to(after[0:1, :], (SUBLANES, T))
            sc_ref[rows, :] = jnp.concatenate(kept, axis=0)
            return later

        lax.fori_loop(0, n_tiles, tile_body, full(0))

    thr_f = thr_ref[...]

    mx_ref[...] = jnp.full((SUBLANES, HT), -jnp.inf, F32)

    def logit_tiles(first, g):
        mx = mx_ref[...]
        for t in range(g):
            kt = first + t
            r0 = pl.multiple_of(kt * T, T)
            l = _dot(key_tile(kt), qt_ref[...])
            sc = sc_ref[pl.ds(r0, T), :].reshape(trows, SUBLANES, T)
            bias = jnp.where(sc >= thr_f[None], 0.0, -jnp.inf).reshape(T, T)
            new = []
            for h in range(DSA_HEADS):
                lm = l[:, T * h:T * (h + 1)] + bias
                lg_ref[pl.ds(r0, T), T * h:T * (h + 1)] = lm
                new.append(jnp.max(lm.reshape(trows, SUBLANES, T), axis=0))
            mx = jnp.maximum(mx, jnp.concatenate(new, axis=1))
        mx_ref[...] = mx

    _for_tile_groups(n_tiles, logit_tiles)
    mrow = jnp.broadcast_to(jnp.max(mx_ref[...], axis=0, keepdims=True), (SUBLANES, HT))

    acc_ref[...] = jnp.zeros(acc_ref.shape, F32)

    def pv_tiles(first, g):
        r0 = pl.multiple_of(first * T, T)
        ps = []
        for t in range(g):
            lm = lg_ref[pl.ds(pl.multiple_of((first + t) * T, T), T), :]
            p = jnp.exp2(lm.reshape(trows, SUBLANES, HT) - mrow[None])
            ps.append(p.reshape(T, HT).astype(BF16))
        p_all = ps[0] if g == 1 else jnp.concatenate(ps, axis=0)
        acc_ref[...] += _dot(ckvt_ref[0, :, pl.ds(r0, g * T)], p_all)

    _for_tile_groups(n_tiles, pv_tiles)
    linv = 1.0 / acc_ref[DSA_LATENT:DSA_LATENT + 1, :]
    olat = (acc_ref[0:DSA_LATENT, :] * linv).astype(BF16)
    outs = [_dot(wuvt_ref[h], olat[:, T * h:T * (h + 1)]) for h in range(DSA_HEADS)]
    o = jnp.concatenate(outs, axis=0).T
    o_ref[0] = (o * gate_ref[0].astype(F32)).astype(BF16)


def _dsa(keys, ckvt, iqt, dqt, iwt, w_uk, w_uv, dgate):
    B, S, _ = keys.shape
    T = DSA_QB
    assert S % T == 0 and T % CHUNK == 0
    n_qblk = S // T
    n_sel = min(TOPK_MAX, S // 4)
    H = DSA_HEADS
    wukt = jnp.concatenate([jnp.zeros((H, DSA_LATENT, DSA_ROPE), F32), jnp.swapaxes(w_uk, 1, 2)], axis=2).astype(BF16)
    wuvt = jnp.swapaxes(w_uv, 1, 2).astype(BF16)
    HT = H * T
    per_b = lambda b, j: (b, 0, 0)
    qcol = lambda b, j: (b, 0, j)
    full3 = lambda b, j: (0, 0, 0)
    return pl.pallas_call(
        functools.partial(_dsa_kernel, n_sel=n_sel, n_qblk=n_qblk),
        grid=(B, n_qblk),
        in_specs=[
            pl.BlockSpec((1, S, KEY_W), per_b),
            pl.BlockSpec((1, CKVT_ROWS, S), per_b),
            pl.BlockSpec((1, IDX_HEADS * IDX_DIM, T), qcol),
            pl.BlockSpec((1, H * DSA_HEAD_DIM, T), qcol),
            pl.BlockSpec((1, IDX_HEADS, T), qcol),
            pl.BlockSpec((H, DSA_LATENT, DSA_HEAD_DIM), full3),
            pl.BlockSpec((H, DSA_VDIM, DSA_LATENT), full3),
            pl.BlockSpec((1, T, H * DSA_VDIM), lambda b, j: (b, j, 0)),
        ],
        out_specs=pl.BlockSpec((1, T, H * DSA_VDIM), lambda b, j: (b, j, 0)),
        out_shape=jax.ShapeDtypeStruct((B, S, H * DSA_VDIM), BF16),
        scratch_shapes=[
            pltpu.VMEM((KEY_W, HT), BF16),
            pltpu.VMEM((KEY_W, HT), BF16),
            pltpu.VMEM((S, T), F32),
            pltpu.VMEM((S, HT), F32),
            pltpu.VMEM((CKVT_ROWS, HT), F32),
            pltpu.VMEM((SUBLANES, T), F32),
            pltpu.VMEM((SUBLANES, T), I32),
            pltpu.VMEM((SUBLANES, HT), F32),
        ],
        compiler_params=pltpu.CompilerParams(
            dimension_semantics=("arbitrary", "arbitrary"), vmem_limit_bytes=VMEM_LIMIT_BYTES),
        name="dsa",
    )(keys, ckvt, iqt, dqt, iwt, wukt, wuvt, dgate)


def _out_kernel(x_ref, ret_ref, dsa_ref, p_ref, wo_ref, gp_ref, wg_ref, wp_ref, gf_ref, o_ref, *, final_norm):
    nr = ret_ref.shape[2]
    x1 = x_ref[0] + _dot(ret_ref[0], wo_ref[0:nr, :]) + _dot(dsa_ref[0], wo_ref[nr:, :])
    ms = jnp.mean(x1 * x1, axis=-1, keepdims=True)
    h = (x1 * lax.rsqrt(ms + EPS) * gp_ref[...]).astype(BF16)
    z = _dot(h, wg_ref[...])
    gate = 1.0 / (1.0 + jnp.exp(-z))
    x2 = x1 + gate * _dot(p_ref[0].astype(BF16), wp_ref[...])
    if final_norm:
        ms2 = jnp.mean(x2 * x2, axis=-1, keepdims=True)
        x2 = x2 * lax.rsqrt(ms2 + EPS) * gf_ref[...]
    o_ref[0] = x2


def _out_proj(x, ret, dsa, p, w_out, norm_ple, w_gate, w_proj, norm_final, tm, final_norm):
    B, S, D = x.shape
    row = lambda b, s: (b, s, 0)
    full2 = lambda b, s: (0, 0)
    return pl.pallas_call(
        functools.partial(_out_kernel, final_norm=final_norm),
        grid=(B, S // tm),
        in_specs=[
            pl.BlockSpec((1, tm, D), row),
            pl.BlockSpec((1, tm, ret.shape[2]), row),
            pl.BlockSpec((1, tm, dsa.shape[2]), row),
            pl.BlockSpec((1, tm, p.shape[2]), row),
            pl.BlockSpec(w_out.shape, full2),
            pl.BlockSpec((1, D), full2),
            pl.BlockSpec(w_gate.shape, full2),
            pl.BlockSpec(w_proj.shape, full2),
            pl.BlockSpec((1, D), full2),
        ],
        out_specs=pl.BlockSpec((1, tm, D), row),
        out_shape=jax.ShapeDtypeStruct((B, S, D), F32),
        compiler_params=pltpu.CompilerParams(
            dimension_semantics=("arbitrary", "arbitrary"), vmem_limit_bytes=VMEM_LIMIT_BYTES),
        name="out_proj",
    )(x, ret, dsa, p, w_out.astype(BF16), norm_ple.reshape(1, D), w_gate.astype(BF16),
      w_proj.astype(BF16), norm_final.reshape(1, D))


def _tiles(S):
    tm = math.gcd(S, 512)
    blk = math.gcd(S, 256)
    return tm, blk


def kernel(x, p, positions, norm_mix, w_in, ret_norm, kv_norm, w_uk, w_uv, w_out, norm_ple, w_ple_gate, w_ple_proj, norm_final):
    B, S, D = x.shape
    depth = w_in.shape[0]
    tm, blk = _tiles(S)
    assert blk % CHUNK == 0 and (1 << CHUNK_SHIFT) == CHUNK
    for i in range(depth):
        rq, rk, rv, rg, dg, keys, ckvt, iqt, dqt, iwt = _in_proj(x, positions, norm_mix[i], w_in[i], kv_norm[i], tm)
        ret = _retention(rq, rk, rv, rg, ret_norm[i], blk)
        dsa = _dsa(keys, ckvt, iqt, dqt, iwt, w_uk[i], w_uv[i], dg)
        x = _out_proj(x, ret, dsa, p[i], w_out[i], norm_ple[i], w_ple_gate[i], w_ple_proj[i],
                      norm_final, tm, final_norm=(i == depth - 1))
    return x
```

```python
import functools
import math

import numpy as np
import jax
import jax.numpy as jnp
from jax import lax
from jax.experimental import pallas as pl
from jax.experimental.pallas import tpu as pltpu

F32 = jnp.float32
BF16 = jnp.bfloat16
I32 = jnp.int32

LANES = 128
SUBLANES = 8
VMEM_LIMIT_BYTES = 48 * 1024 * 1024

CHUNK = 64
CHUNK_SHIFT = 6
DSA_QB = 256
EPS = 1e-6
ROPE_THETA = 500000.0
RET_HEADS = 8
RET_DK = 64
RET_THETA = 10000.0
DSA_HEADS = 8
DSA_HEAD_DIM = 64
DSA_ROPE = 16
DSA_LATENT = 128
DSA_VDIM = 64
TOPK_MAX = 256
IDX_HEADS = 8
IDX_DIM = 32
IDX_ROPE = 8
IN_SPLITS = (512, 512, 512, 512, 512, 128, 16, 512, 256, 32, 8)

KPE_LANE = 0
KIDX_LANE = 32
KEY_W = 2 * LANES
CKVT_ROWS = DSA_LATENT + 16

INT_MIN = int(np.iinfo(np.int32).min)
INT_MAX = int(np.iinfo(np.int32).max)
NEG_INF_KEY = INT_MIN + 0x7FFFFF
F32_LOWEST = float(np.finfo(np.float32).min)

NT_DIMS = (((1,), (1,)), ((), ()))
LOG2E = math.log2(math.e)
PACKED_SUBLANES = 16
COARSE_BITS = 16


def _dot(a, b):
    return jnp.dot(a, b, preferred_element_type=F32)


def _dot_nt(a, b):
    return lax.dot_general(a, b, NT_DIMS, preferred_element_type=F32)


def _silu(y):
    return y / (1.0 + jnp.exp(-y))


def _in_proj_kernel(x_ref, g_ref, wn_ref, wt_ref, kvg_ref, kvgc_ref,
                    cosr_ref, sinr_ref, coss_ref, sins_ref,
                    cosit_ref, sinit_ref, cosqt_ref, sinqt_ref,
                    rq_ref, rk_ref, rv_ref, rg_ref, dg_ref, keys_ref,
                    ckvt_ref, iqt_ref, dqt_ref, iwt_ref):
    x = x_ref[0]
    tm = x.shape[0]
    ms = jnp.mean(x * x, axis=-1, keepdims=True)
    h = (x * lax.rsqrt(ms + EPS) * g_ref[...]).astype(BF16)

    def nat(c0, width):
        return _dot(h, wn_ref[:, c0:c0 + width])

    cosr = cosr_ref[...]
    sinr = sinr_ref[...]
    lane = lax.broadcasted_iota(I32, (tm, LANES), 1)
    first_half = (lane & (RET_DK // 2)) == 0

    def rope64(y):
        outs = []
        for g in range(y.shape[1] // LANES):
            yg = y[:, LANES * g:LANES * (g + 1)]
            partner = jnp.where(first_half, pltpu.roll(yg, LANES - RET_DK // 2, 1),
                                pltpu.roll(yg, RET_DK // 2, 1))
            outs.append(yg * cosr + partner * sinr)
        return jnp.concatenate(outs, axis=1)

    rw = RET_HEADS * RET_DK
    rq_ref[0] = (rope64(nat(0, rw)) * (RET_DK ** -0.5)).astype(BF16)
    rk_ref[0] = rope64(nat(rw, rw)).astype(BF16)
    rv_ref[0] = nat(2 * rw, rw).astype(BF16)
    rg_ref[0] = _silu(nat(3 * rw, rw)).astype(BF16)
    dg_ref[0] = _silu(nat(4 * rw, rw)).astype(BF16)
    c0 = 5 * rw
    dc = nat(c0, DSA_LATENT)
    ckv = dc * lax.rsqrt(jnp.mean(dc * dc, axis=-1, keepdims=True) + EPS) * kvg_ref[...]
    keys_ref[0, :, 0:LANES] = ckv.astype(BF16)
    small = nat(c0 + LANES, LANES) * coss_ref[...] + nat(c0 + 2 * LANES, LANES) * sins_ref[...]
    keys_ref[0, :, LANES:KEY_W] = small.astype(BF16)

    r = _dot_nt(wt_ref[...], h)
    cit = cosit_ref[...]
    sit = sinit_ref[...]
    pieces = []
    for hh in range(IDX_HEADS):
        base = IDX_DIM * hh
        xr = r[base:base + IDX_ROPE]
        pieces.append(xr * cit + pltpu.roll(xr, IDX_ROPE // 2, 0) * sit)
        pieces.append(r[base + IDX_ROPE:base + IDX_DIM])
    iqt_ref[0] = jnp.concatenate(pieces, axis=0).astype(BF16)

    cqt = cosqt_ref[...]
    sqt = sinqt_ref[...]
    half = DSA_ROPE // 2
    off = IDX_HEADS * IDX_DIM
    pieces = []
    for hh in range(DSA_HEADS):
        base = off + DSA_HEAD_DIM * hh
        x1 = r[base:base + half]
        x2 = r[base + half:base + DSA_ROPE]
        pieces.append(x1 * cqt - x2 * sqt)
        pieces.append(x2 * cqt + x1 * sqt)
        pieces.append(r[base + DSA_ROPE:base + DSA_HEAD_DIM])
    dqt_ref[0] = jnp.concatenate(pieces, axis=0).astype(BF16)

    off += DSA_HEADS * DSA_HEAD_DIM
    dct = r[off:off + DSA_LATENT]
    ckvt = dct * lax.rsqrt(jnp.mean(dct * dct, axis=0, keepdims=True) + EPS) * kvgc_ref[...]
    extra = jnp.where(lax.broadcasted_iota(I32, (CKVT_ROWS - DSA_LATENT, tm), 0) == 0, 1.0, 0.0)
    ckvt_ref[0] = jnp.concatenate([ckvt, extra], axis=0).astype(BF16)
    off += DSA_LATENT
    iwt_ref[0] = r[off:off + IDX_HEADS] * (IDX_HEADS ** -0.5 * IDX_DIM ** -0.5)


def _rope_angles(positions, r, theta):
    inv = theta ** (-jnp.arange(0, r, 2, dtype=F32) / r)
    return positions.astype(F32)[:, None] * inv[None, :]


def _in_proj(x, positions, norm_mix, w_in, kv_norm, tm):
    B, S, D = x.shape
    offs = np.cumsum((0,) + IN_SPLITS)
    w_rq, w_rk, w_rv, w_rg, w_dq, w_dc, w_dpe, w_dg, w_iq, w_ik, w_iw = (
        w_in[:, offs[i]:offs[i + 1]] for i in range(len(IN_SPLITS)))

    hp, hi = DSA_ROPE // 2, IDX_ROPE // 2
    small_a = jnp.zeros((D, LANES), F32)
    small_a = small_a.at[:, KPE_LANE:KPE_LANE + DSA_ROPE].set(w_dpe)
    small_a = small_a.at[:, KIDX_LANE:KIDX_LANE + IDX_DIM].set(w_ik)
    small_b = jnp.zeros((D, LANES), F32)
    small_b = small_b.at[:, KPE_LANE:KPE_LANE + hp].set(-w_dpe[:, hp:])
    small_b = small_b.at[:, KPE_LANE + hp:KPE_LANE + DSA_ROPE].set(w_dpe[:, :hp])
    small_b = small_b.at[:, KIDX_LANE:KIDX_LANE + hi].set(-w_ik[:, hi:IDX_ROPE])
    small_b = small_b.at[:, KIDX_LANE + hi:KIDX_LANE + IDX_ROPE].set(w_ik[:, :hi])
    wn = jnp.concatenate([w_rq, w_rk, w_rv, w_rg, w_dg, w_dc, small_a, small_b], axis=1).astype(BF16)
    wt = jnp.concatenate([w_iq.T, w_dq.T, w_dc.T, w_iw.T, jnp.zeros((SUBLANES, D), F32)], axis=0).astype(BF16)
    n_nat = wn.shape[1]
    n_t = wt.shape[0]

    ang_r = _rope_angles(positions, RET_DK, RET_THETA)
    cos_r = jnp.tile(jnp.cos(ang_r), (1, 4))
    sin_r = jnp.tile(jnp.concatenate([-jnp.sin(ang_r), jnp.sin(ang_r)], axis=1), (1, 2))
    ang_p = _rope_angles(positions, DSA_ROPE, ROPE_THETA)
    ang_i = _rope_angles(positions, IDX_ROPE, ROPE_THETA)
    cos_s = jnp.ones((S, LANES), F32)
    cos_s = cos_s.at[:, KPE_LANE:KPE_LANE + DSA_ROPE].set(jnp.tile(jnp.cos(ang_p), (1, 2)))
    cos_s = cos_s.at[:, KIDX_LANE:KIDX_LANE + IDX_ROPE].set(jnp.tile(jnp.cos(ang_i), (1, 2)))
    sin_s = jnp.zeros((S, LANES), F32)
    sin_s = sin_s.at[:, KPE_LANE:KPE_LANE + DSA_ROPE].set(jnp.tile(jnp.sin(ang_p), (1, 2)))
    sin_s = sin_s.at[:, KIDX_LANE:KIDX_LANE + IDX_ROPE].set(jnp.tile(jnp.sin(ang_i), (1, 2)))
    cos_it = jnp.tile(jnp.cos(ang_i).T, (2, 1))
    sin_it = jnp.concatenate([-jnp.sin(ang_i).T, jnp.sin(ang_i).T], axis=0)
    cos_qt = jnp.cos(ang_p).T
    sin_qt = jnp.sin(ang_p).T

    rw = RET_HEADS * RET_DK
    row = lambda b, s: (b, s, 0)
    col = lambda b, s: (b, 0, s)
    full2 = lambda b, s: (0, 0)
    tok_tab = pl.BlockSpec((tm, LANES), lambda b, s: (s, 0))
    col_tab = pl.BlockSpec((SUBLANES, tm), lambda b, s: (0, s))
    out_shape = (
        jax.ShapeDtypeStruct((B, S, rw), BF16),
        jax.ShapeDtypeStruct((B, S, rw), BF16),
        jax.ShapeDtypeStruct((B, S, rw), BF16),
        jax.ShapeDtypeStruct((B, S, rw), BF16),
        jax.ShapeDtypeStruct((B, S, rw), BF16),
        jax.ShapeDtypeStruct((B, S, KEY_W), BF16),
        jax.ShapeDtypeStruct((B, CKVT_ROWS, S), BF16),
        jax.ShapeDtypeStruct((B, IDX_HEADS * IDX_DIM, S), BF16),
        jax.ShapeDtypeStruct((B, DSA_HEADS * DSA_HEAD_DIM, S), BF16),
        jax.ShapeDtypeStruct((B, IDX_HEADS, S), F32),
    )
    out_specs = (
        pl.BlockSpec((1, tm, rw), row), pl.BlockSpec((1, tm, rw), row), pl.BlockSpec((1, tm, rw), row),
        pl.BlockSpec((1, tm, rw), row), pl.BlockSpec((1, tm, rw), row),
        pl.BlockSpec((1, tm, KEY_W), row),
        pl.BlockSpec((1, CKVT_ROWS, tm), col),
        pl.BlockSpec((1, IDX_HEADS * IDX_DIM, tm), col),
        pl.BlockSpec((1, DSA_HEADS * DSA_HEAD_DIM, tm), col),
        pl.BlockSpec((1, IDX_HEADS, tm), col),
    )
    in_specs = [
        pl.BlockSpec((1, tm, D), row),
        pl.BlockSpec((1, D), full2),
        pl.BlockSpec((D, n_nat), full2),
        pl.BlockSpec((n_t, D), full2),
        pl.BlockSpec((1, DSA_LATENT), full2),
        pl.BlockSpec((DSA_LATENT, 1), full2),
        tok_tab, tok_tab, tok_tab, tok_tab, col_tab, col_tab, col_tab, col_tab,
    ]
    return pl.pallas_call(
        _in_proj_kernel,
        grid=(B, S // tm),
        in_specs=in_specs,
        out_specs=out_specs,
        out_shape=out_shape,
        compiler_params=pltpu.CompilerParams(
            dimension_semantics=("arbitrary", "arbitrary"), vmem_limit_bytes=VMEM_LIMIT_BYTES),
        name="in_proj",
    )(x, norm_mix.reshape(1, D), wn, wt, kv_norm.reshape(1, DSA_LATENT), kv_norm.reshape(DSA_LATENT, 1),
      cos_r, sin_r, cos_s, sin_s, cos_it, sin_it, cos_qt, sin_qt)


def _retention_kernel(q_ref, k_ref, v_ref, g_ref, dmat_ref, xi_ref, zeta_ref, gl_ref, nrm_ref,
                      o_ref, state_ref, *, n_blocks, blk):
    lane = lax.broadcasted_iota(I32, (blk, LANES), 1)
    lo = lane < RET_DK
    r2 = lax.broadcasted_iota(I32, (LANES, LANES), 0) < RET_DK
    c2 = lax.broadcasted_iota(I32, (LANES, LANES), 1) < RET_DK
    same_head = r2 == c2
    npair = state_ref.shape[0]
    state_ref[...] = jnp.zeros(state_ref.shape, F32)
    inv_dv = 1.0 / RET_DK

    def one_block(r0):
        for p in range(npair):
            cols = slice(LANES * p, LANES * (p + 1))
            q = q_ref[0, pl.ds(r0, blk), cols]
            k = k_ref[0, pl.ds(r0, blk), cols]
            v = v_ref[0, pl.ds(r0, blk), cols]
            qf = q.astype(F32)
            q0 = jnp.where(lo, qf, 0.0).astype(BF16)
            q1 = jnp.where(lo, 0.0, qf).astype(BF16)
            p0 = (_dot_nt(q0, k) * dmat_ref[p, 0]).astype(BF16)
            p1 = (_dot_nt(q1, k) * dmat_ref[p, 1]).astype(BF16)
            o_intra = jnp.where(lo, _dot(p0, v), _dot(p1, v))
            st = state_ref[p]
            o_cross = _dot((qf * xi_ref[p]).astype(BF16), st.astype(BF16))
            o = o_intra + o_cross
            kzt = (k.astype(F32) * zeta_ref[p]).T.astype(BF16)
            state_ref[p] = st * gl_ref[p] + jnp.where(same_head, _dot(kzt, v), 0.0)

            s_lo = jnp.sum(jnp.where(lo, o, 0.0), axis=-1, keepdims=True)
            s_hi = jnp.sum(jnp.where(lo, 0.0, o), axis=-1, keepdims=True)
            d = o - jnp.where(lo, s_lo, s_hi) * inv_dv
            d2 = d * d
            v_lo = jnp.sum(jnp.where(lo, d2, 0.0), axis=-1, keepdims=True)
            v_hi = jnp.sum(jnp.where(lo, 0.0, d2), axis=-1, keepdims=True)
            var = jnp.where(lo, v_lo, v_hi) * inv_dv
            y = d * lax.rsqrt(var + EPS) * nrm_ref[p] * g_ref[0, pl.ds(r0, blk), cols].astype(F32)
            o_ref[0, pl.ds(r0, blk), cols] = y.astype(BF16)

    per_trip = next(u for u in (4, 2, 1) if n_blocks % u == 0)

    def body(i, carry):
        for u in range(per_trip):
            one_block(pl.multiple_of((i * per_trip + u) * blk, blk))
        return carry

    lax.fori_loop(0, n_blocks // per_trip, body, 0)


def _retention_tables(blk):
    hs = np.arange(RET_HEADS, dtype=np.float64)
    log_g = np.log1p(-np.exp2(-5.0 - hs))
    t = np.arange(blk, dtype=np.float64)
    dist = np.abs(t[:, None] - t[None, :])
    causal = (t[None, :] // CHUNK) <= (t[:, None] // CHUNK)
    dmat = np.where(causal[None], np.exp(log_g[:, None, None] * dist[None]), 0.0)
    xi = np.exp(log_g[:, None] * (t + 1.0)[None, :])
    zeta = np.exp(log_g[:, None] * (blk - 1.0 - t)[None, :])
    gl = np.exp(log_g * blk)
    npair = RET_HEADS // 2

    def lanes(a):
        a = a.reshape(npair, 2, blk)
        return np.repeat(np.transpose(a, (0, 2, 1)), RET_DK, axis=2)

    gl_l = np.repeat(gl.reshape(npair, 1, 2), RET_DK, axis=2)
    f = lambda a: jnp.asarray(a, F32)
    return f(dmat.reshape(npair, 2, blk, blk)), f(lanes(xi)), f(lanes(zeta)), f(gl_l)


def _retention(rq, rk, rv, rg, ret_norm, blk):
    B, S, W = rq.shape
    npair = W // LANES
    dmat, xi, zeta, gl = _retention_tables(blk)
    seq = pl.BlockSpec((1, S, W), lambda b: (b, 0, 0))
    tab = pl.BlockSpec((npair, blk, LANES), lambda b: (0, 0, 0))
    vec = pl.BlockSpec((npair, 1, LANES), lambda b: (0, 0, 0))
    return pl.pallas_call(
        functools.partial(_retention_kernel, n_blocks=S // blk, blk=blk),
        grid=(B,),
        in_specs=[seq, seq, seq, seq,
                  pl.BlockSpec((npair, 2, blk, blk), lambda b: (0, 0, 0, 0)),
                  tab, tab, vec, vec],
        out_specs=seq,
        out_shape=jax.ShapeDtypeStruct((B, S, W), BF16),
        scratch_shapes=[pltpu.VMEM((npair, LANES, LANES), F32)],
        compiler_params=pltpu.CompilerParams(
            dimension_semantics=("arbitrary",), vmem_limit_bytes=VMEM_LIMIT_BYTES),
        name="retention",
    )(rq, rk, rv, rg, dmat, xi, zeta, gl, ret_norm.reshape(npair, 1, LANES))


def _for_tile_groups(n_tiles, body):
    n4 = n_tiles >> 2

    def loop_body(c, carry):
        body(c * 4, 4)
        return carry

    lax.fori_loop(0, n4, loop_body, 0)

    @pl.when((n_tiles & 2) != 0)
    def _():
        body(n4 * 4, 2)

    @pl.when((n_tiles & 1) != 0)
    def _():
        body(n4 * 4 + (n_tiles & 2), 1)


def _dsa_kernel(keys_ref, ckvt_ref, iqt_ref, dqt_ref, iwt_ref, wukt_ref, wuvt_ref, gate_ref,
                o_ref, qit_ref, qt_ref, sc_ref, kc_ref, lg_ref, acc_ref, thr_ref, cnt_ref, mx_ref,
                *, n_sel, n_qblk):
    j = pl.program_id(1)
    T = DSA_QB
    HT = DSA_HEADS * T
    n_tiles = j + 1
    scale = DSA_HEAD_DIM ** -0.5 * LOG2E

    @pl.when(j == 0)
    def _():
        qit_ref[...] = jnp.zeros(qit_ref.shape, BF16)
        qt_ref[...] = jnp.zeros(qt_ref.shape, BF16)

    for h in range(DSA_HEADS):
        cols = slice(T * h, T * (h + 1))
        r0 = LANES + KIDX_LANE
        qit_ref[r0:r0 + IDX_DIM, cols] = iqt_ref[0, IDX_DIM * h:IDX_DIM * (h + 1), :]
        dq = dqt_ref[0, DSA_HEAD_DIM * h:DSA_HEAD_DIM * (h + 1), :]
        qlat = _dot(wukt_ref[h], dq) * scale
        qt_ref[0:DSA_LATENT, cols] = qlat.astype(BF16)
        r1 = LANES + KPE_LANE
        qt_ref[r1:r1 + DSA_ROPE, cols] = (dq[0:DSA_ROPE].astype(F32) * scale).astype(BF16)

    def key_tile(kt):
        return keys_ref[0, pl.ds(pl.multiple_of(kt * T, T), T), :]

    w = iwt_ref[0]

    def score_tiles(first, g):
        for t in range(g):
            kt = first + t
            s = _dot(key_tile(kt), qit_ref[...])
            score = jnp.zeros((T, T), F32)
            for h in range(IDX_HEADS):
                score = score + jnp.maximum(s[:, T * h:T * (h + 1)], 0.0) * w[h:h + 1, :]
            rows = pl.ds(pl.multiple_of(kt * T, T), T)
            sc_ref[rows, :] = score
            kc_ref[rows, :] = score.astype(BF16)

    _for_tile_groups(n_tiles, score_tiles)
    r_hid = pl.multiple_of(j * T + CHUNK, CHUNK)
    key_chunk = (lax.broadcasted_iota(I32, (T - CHUNK, T), 0) >> CHUNK_SHIFT) + 1
    query_chunk = lax.broadcasted_iota(I32, (T - CHUNK, T), 1) >> CHUNK_SHIFT
    masked = jnp.where(key_chunk > query_chunk, -jnp.inf, sc_ref[pl.ds(r_hid, T - CHUNK), :])
    sc_ref[pl.ds(r_hid, T - CHUNK), :] = masked
    kc_ref[pl.ds(r_hid, T - CHUNK), :] = masked.astype(BF16)

    trows = T // SUBLANES
    full = lambda v: jnp.full((SUBLANES, T), v, I32)
    one_c = jnp.ones((PACKED_SUBLANES, T), BF16)
    zero_c = jnp.zeros((PACKED_SUBLANES, T), BF16)
    sub_total = lambda acc: jnp.broadcast_to(jnp.sum(acc, axis=0, keepdims=True), (SUBLANES, T))
    key_to_float = lambda key: lax.bitcast_convert_type(key ^ ((key >> 31) & INT_MAX), F32)
    thr_ref[...] = jnp.full((SUBLANES, T), F32_LOWEST, F32)
    cnt_ref[...] = full(n_sel)

    def descend(nt):
        def update(c, cand, cnt):
            thr, cnt_thr = c
            take = cnt >= n_sel
            return jnp.where(take, cand, thr), jnp.where(take, cnt, cnt_thr)

        def coarse_step(i, c):
            cand = c[0] + lax.shift_left(jnp.int32(1), 31 - i)
            hi = cand >> 16
            grid = lax.bitcast_convert_type((hi ^ ((hi >> 31) & 0x7FFF)) << 16, F32).astype(BF16)
            grid = jnp.broadcast_to(grid[0:1, :], (PACKED_SUBLANES, T))
            accs = [zero_c] * 4
            for idx in range(nt * T // PACKED_SUBLANES):
                row = PACKED_SUBLANES * idx
                accs[idx % 4] = accs[idx % 4] + jnp.where(kc_ref[row:row + PACKED_SUBLANES, :] >= grid, one_c, zero_c)
            total = (accs[0].astype(F32) + accs[1].astype(F32)) + (accs[2].astype(F32) + accs[3].astype(F32))
            cnt = jnp.sum(total, axis=0, keepdims=True).astype(I32)
            return update(c, cand, jnp.broadcast_to(cnt, (SUBLANES, T)))

        def fine_step(i, c):
            cand = c[0] + lax.shift_left(jnp.int32(1), 31 - i)
            cand_f = key_to_float(cand)
            accs = [full(0)] * 4
            for idx in range(nt * trows):
                row = SUBLANES * idx
                accs[idx % 4] = accs[idx % 4] + jnp.where(sc_ref[row:row + SUBLANES, :] >= cand_f, 1, 0)
            return update(c, cand, sub_total((accs[0] + accs[1]) + (accs[2] + accs[3])))

        start = (full(INT_MIN), full(nt * T))
        coarse, _ = lax.fori_loop(0, COARSE_BITS, coarse_step, start)
        base = jnp.where(coarse <= NEG_INF_KEY, INT_MIN, coarse - (1 << (32 - COARSE_BITS)))
        thr, cnt_thr = lax.fori_loop(COARSE_BITS - 2, 32, fine_step, (base, full(nt * T)))
        few = thr <= NEG_INF_KEY
        thr_ref[...] = jnp.where(few, F32_LOWEST, key_to_float(thr))
        cnt_ref[...] = jnp.where(few, n_sel, cnt_thr)

    for nt in range(n_sel // T + 1, n_qblk + 1):
        pl.when(n_tiles == nt)(functools.partial(descend, nt))

    @pl.when(jnp.max(jnp.where(cnt_ref[...] > n_sel, 1, 0)) > 0)
    def _demote_excess_ties():
        thr = thr_ref[...]
        excess = cnt_ref[...] - n_sel
        sub = lax.broadcasted_iota(I32, (SUBLANES, T), 0)

        def tile_body(i, later):
            rows = pl.ds(pl.multiple_of((n_tiles - 1 - i) * T, T), T)
            tile = sc_ref[rows, :]
            kept = [None] * trows
            for v in range(trows - 1, -1, -1):
                sc = tile[SUBLANES * v:SUBLANES * (v + 1), :]
                tied = jnp.where(sc == thr, 1, 0)
                after = tied
                for k in (1, 2, 4):
                    after = after + jnp.where(sub < SUBLANES - k, pltpu.roll(after, SUBLANES - k, 0), 0)
                rank_from_end = later + after - tied
                drop = jnp.where(tied > 0, jnp.where(rank_from_end < excess, 1, 0), 0)
                kept[v] = jnp.where(drop > 0, -jnp.inf, sc)
                later = later + jnp.broadcast_to(after[0:1, :], (SUBLANES, T))
            sc_ref[rows, :] = jnp.concatenate(kept, axis=0)
            return later

        lax.fori_loop(0, n_tiles, tile_body, full(0))

    thr_f = thr_ref[...]

    mx_ref[...] = jnp.full((SUBLANES, HT), -jnp.inf, F32)

    def logit_tiles(first, g):
        mx = mx_ref[...]
        for t in range(g):
            kt = first + t
            r0 = pl.multiple_of(kt * T, T)
            l = _dot(key_tile(kt), qt_ref[...])
            sc = sc_ref[pl.ds(r0, T), :].reshape(trows, SUBLANES, T)
            bias = jnp.where(sc >= thr_f[None], 0.0, -jnp.inf).reshape(T, T)
            new = []
            for h in range(DSA_HEADS):
                lm = l[:, T * h:T * (h + 1)] + bias
                lg_ref[pl.ds(r0, T), T * h:T * (h + 1)] = lm
                new.append(jnp.max(lm.reshape(trows, SUBLANES, T), axis=0))
            mx = jnp.maximum(mx, jnp.concatenate(new, axis=1))
        mx_ref[...] = mx

    _for_tile_groups(n_tiles, logit_tiles)
    mrow = jnp.broadcast_to(jnp.max(mx_ref[...], axis=0, keepdims=True), (SUBLANES, HT))

    acc_ref[...] = jnp.zeros(acc_ref.shape, F32)

    def pv_tiles(first, g):
        r0 = pl.multiple_of(first * T, T)
        ps = []
        for t in range(g):
            lm = lg_ref[pl.ds(pl.multiple_of((first + t) * T, T), T), :]
            p = jnp.exp2(lm.reshape(trows, SUBLANES, HT) - mrow[None])
            ps.append(p.reshape(T, HT).astype(BF16))
        p_all = ps[0] if g == 1 else jnp.concatenate(ps, axis=0)
        acc_ref[...] += _dot(ckvt_ref[0, :, pl.ds(r0, g * T)], p_all)

    _for_tile_groups(n_tiles, pv_tiles)
    linv = 1.0 / acc_ref[DSA_LATENT:DSA_LATENT + 1, :]
    olat = (acc_ref[0:DSA_LATENT, :] * linv).astype(BF16)
    outs = [_dot(wuvt_ref[h], olat[:, T * h:T * (h + 1)]) for h in range(DSA_HEADS)]
    o = jnp.concatenate(outs, axis=0).T
    o_ref[0] = (o * gate_ref[0].astype(F32)).astype(BF16)


def _dsa(keys, ckvt, iqt, dqt, iwt, w_uk, w_uv, dgate):
    B, S, _ = keys.shape
    T = DSA_QB
    assert S % T == 0 and T % CHUNK == 0
    n_qblk = S // T
    n_sel = min(TOPK_MAX, S // 4)
    H = DSA_HEADS
    wukt = jnp.concatenate([jnp.zeros((H, DSA_LATENT, DSA_ROPE), F32), jnp.swapaxes(w_uk, 1, 2)], axis=2).astype(BF16)
    wuvt = jnp.swapaxes(w_uv, 1, 2).astype(BF16)
    HT = H * T
    per_b = lambda b, j: (b, 0, 0)
    qcol = lambda b, j: (b, 0, j)
    full3 = lambda b, j: (0, 0, 0)
    return pl.pallas_call(
        functools.partial(_dsa_kernel, n_sel=n_sel, n_qblk=n_qblk),
        grid=(B, n_qblk),
        in_specs=[
            pl.BlockSpec((1, S, KEY_W), per_b),
            pl.BlockSpec((1, CKVT_ROWS, S), per_b),
            pl.BlockSpec((1, IDX_HEADS * IDX_DIM, T), qcol),
            pl.BlockSpec((1, H * DSA_HEAD_DIM, T), qcol),
            pl.BlockSpec((1, IDX_HEADS, T), qcol),
            pl.BlockSpec((H, DSA_LATENT, DSA_HEAD_DIM), full3),
            pl.BlockSpec((H, DSA_VDIM, DSA_LATENT), full3),
            pl.BlockSpec((1, T, H * DSA_VDIM), lambda b, j: (b, j, 0)),
        ],
        out_specs=pl.BlockSpec((1, T, H * DSA_VDIM), lambda b, j: (b, j, 0)),
        out_shape=jax.ShapeDtypeStruct((B, S, H * DSA_VDIM), BF16),
        scratch_shapes=[
            pltpu.VMEM((KEY_W, HT), BF16),
            pltpu.VMEM((KEY_W, HT), BF16),
            pltpu.VMEM((S, T), F32),
            pltpu.VMEM((S, T), BF16),
            pltpu.VMEM((S, HT), F32),
            pltpu.VMEM((CKVT_ROWS, HT), F32),
            pltpu.VMEM((SUBLANES, T), F32),
            pltpu.VMEM((SUBLANES, T), I32),
            pltpu.VMEM((SUBLANES, HT), F32),
        ],
        compiler_params=pltpu.CompilerParams(
            dimension_semantics=("arbitrary", "arbitrary"), vmem_limit_bytes=VMEM_LIMIT_BYTES),
        name="dsa",
    )(keys, ckvt, iqt, dqt, iwt, wukt, wuvt, dgate)


def _out_kernel(x_ref, ret_ref, dsa_ref, p_ref, wo_ref, gp_ref, wg_ref, wp_ref, gf_ref, o_ref, *, final_norm):
    nr = ret_ref.shape[2]
    x1 = x_ref[0] + _dot(ret_ref[0], wo_ref[0:nr, :]) + _dot(dsa_ref[0], wo_ref[nr:, :])
    ms = jnp.mean(x1 * x1, axis=-1, keepdims=True)
    h = (x1 * lax.rsqrt(ms + EPS) * gp_ref[...]).astype(BF16)
    z = _dot(h, wg_ref[...])
    gate = 1.0 / (1.0 + jnp.exp(-z))
    x2 = x1 + gate * _dot(p_ref[0].astype(BF16), wp_ref[...])
    if final_norm:
        ms2 = jnp.mean(x2 * x2, axis=-1, keepdims=True)
        x2 = x2 * lax.rsqrt(ms2 + EPS) * gf_ref[...]
    o_ref[0] = x2


def _out_proj(x, ret, dsa, p, w_out, norm_ple, w_gate, w_proj, norm_final, tm, final_norm):
    B, S, D = x.shape
    row = lambda b, s: (b, s, 0)
    full2 = lambda b, s: (0, 0)
    return pl.pallas_call(
        functools.partial(_out_kernel, final_norm=final_norm),
        grid=(B, S // tm),
        in_specs=[
            pl.BlockSpec((1, tm, D), row),
            pl.BlockSpec((1, tm, ret.shape[2]), row),
            pl.BlockSpec((1, tm, dsa.shape[2]), row),
            pl.BlockSpec((1, tm, p.shape[2]), row),
            pl.BlockSpec(w_out.shape, full2),
            pl.BlockSpec((1, D), full2),
            pl.BlockSpec(w_gate.shape, full2),
            pl.BlockSpec(w_proj.shape, full2),
            pl.BlockSpec((1, D), full2),
        ],
        out_specs=pl.BlockSpec((1, tm, D), row),
        out_shape=jax.ShapeDtypeStruct((B, S, D), F32),
        compiler_params=pltpu.CompilerParams(
            dimension_semantics=("arbitrary", "arbitrary"), vmem_limit_bytes=VMEM_LIMIT_BYTES),
        name="out_proj",
    )(x, ret, dsa, p, w_out.astype(BF16), norm_ple.reshape(1, D), w_gate.astype(BF16),
      w_proj.astype(BF16), norm_final.reshape(1, D))


def _tiles(S):
    tm = math.gcd(S, 512)
    blk = math.gcd(S, 256)
    return tm, blk


def kernel(x, p, positions, norm_mix, w_in, ret_norm, kv_norm, w_uk, w_uv, w_out, norm_ple, w_ple_gate, w_ple_proj, norm_final):
    B, S, D = x.shape
    depth = w_in.shape[0]
    tm, blk = _tiles(S)
    assert blk % CHUNK == 0 and (1 << CHUNK_SHIFT) == CHUNK
    for i in range(depth):
        rq, rk, rv, rg, dg, keys, ckvt, iqt, dqt, iwt = _in_proj(x, positions, norm_mix[i], w_in[i], kv_norm[i], tm)
        ret = _retention(rq, rk, rv, rg, ret_norm[i], blk)
        dsa = _dsa(keys, ckvt, iqt, dqt, iwt, w_uk[i], w_uv[i], dg)
        x = _out_proj(x, ret, dsa, p[i], w_out[i], norm_ple[i], w_ple_gate[i], w_ple_proj[i],
                      norm_final, tm, final_norm=(i == depth - 1))
    return x
```

```python
import functools
import math

import numpy as np
import jax
import jax.numpy as jnp
from jax import lax
from jax.experimental import pallas as pl
from jax.experimental.pallas import tpu as pltpu

F32 = jnp.float32
BF16 = jnp.bfloat16
I32 = jnp.int32

LANES = 128
SUBLANES = 8
VMEM_LIMIT_BYTES = 48 * 1024 * 1024

CHUNK = 64
CHUNK_SHIFT = 6
DSA_QB = 256
EPS = 1e-6
ROPE_THETA = 500000.0
RET_HEADS = 8
RET_DK = 64
RET_THETA = 10000.0
DSA_HEADS = 8
DSA_HEAD_DIM = 64
DSA_ROPE = 16
DSA_LATENT = 128
DSA_VDIM = 64
TOPK_MAX = 256
IDX_HEADS = 8
IDX_DIM = 32
IDX_ROPE = 8
IN_SPLITS = (512, 512, 512, 512, 512, 128, 16, 512, 256, 32, 8)

KPE_LANE = 0
KIDX_LANE = 32
PARTNER_SHIFT = 64
KEY_W = 2 * LANES
CKVT_ROWS = DSA_LATENT + 16

INT_MIN = int(np.iinfo(np.int32).min)
INT_MAX = int(np.iinfo(np.int32).max)
NEG_INF_KEY = INT_MIN + 0x7FFFFF
F32_LOWEST = float(np.finfo(np.float32).min)

NT_DIMS = (((1,), (1,)), ((), ()))
LOG2E = math.log2(math.e)
PACKED_SUBLANES = 16
COARSE_BITS = 16


def _dot(a, b):
    return jnp.dot(a, b, preferred_element_type=F32)


def _dot_nt(a, b):
    return lax.dot_general(a, b, NT_DIMS, preferred_element_type=F32)


def _silu(y):
    return y / (1.0 + jnp.exp(-y))


def _in_proj_kernel(x_ref, g_ref, wn_ref, wt_ref, kvg_ref, kvgc_ref,
                    cosr_ref, sinr_ref, coss_ref, sins_ref,
                    cosit_ref, sinit_ref, cosqt_ref, sinqt_ref,
                    rq_ref, rk_ref, rv_ref, rg_ref, dg_ref, keys_ref,
                    ckvt_ref, iqt_ref, dqt_ref, iwt_ref):
    x = x_ref[0]
    tm = x.shape[0]
    ms = jnp.mean(x * x, axis=-1, keepdims=True)
    h = (x * lax.rsqrt(ms + EPS) * g_ref[...]).astype(BF16)

    def nat(c0, width):
        return _dot(h, wn_ref[:, c0:c0 + width])

    cosr = cosr_ref[...]
    sinr = sinr_ref[...]
    lane = lax.broadcasted_iota(I32, (tm, LANES), 1)
    first_half = (lane & (RET_DK // 2)) == 0

    def rope64(y):
        outs = []
        for g in range(y.shape[1] // LANES):
            yg = y[:, LANES * g:LANES * (g + 1)]
            partner = jnp.where(first_half, pltpu.roll(yg, LANES - RET_DK // 2, 1),
                                pltpu.roll(yg, RET_DK // 2, 1))
            outs.append(yg * cosr + partner * sinr)
        return jnp.concatenate(outs, axis=1)

    rw = RET_HEADS * RET_DK
    rq_ref[0] = (rope64(nat(0, rw)) * (RET_DK ** -0.5)).astype(BF16)
    rk_ref[0] = rope64(nat(rw, rw)).astype(BF16)
    rv_ref[0] = nat(2 * rw, rw).astype(BF16)
    rg_ref[0] = _silu(nat(3 * rw, rw)).astype(BF16)
    dg_ref[0] = _silu(nat(4 * rw, rw)).astype(BF16)
    c0 = 5 * rw
    dc = nat(c0, DSA_LATENT)
    ckv = dc * lax.rsqrt(jnp.mean(dc * dc, axis=-1, keepdims=True) + EPS) * kvg_ref[...]
    keys_ref[0, :, 0:LANES] = ckv.astype(BF16)
    small = nat(c0 + LANES, LANES)
    small = small * coss_ref[...] + pltpu.roll(small, LANES - PARTNER_SHIFT, 1) * sins_ref[...]
    keys_ref[0, :, LANES:KEY_W] = small.astype(BF16)

    r = _dot_nt(wt_ref[...], h)
    cit = cosit_ref[...]
    sit = sinit_ref[...]
    pieces = []
    for hh in range(IDX_HEADS):
        base = IDX_DIM * hh
        xr = r[base:base + IDX_ROPE]
        pieces.append(xr * cit + pltpu.roll(xr, IDX_ROPE // 2, 0) * sit)
        pieces.append(r[base + IDX_ROPE:base + IDX_DIM])
    iqt_ref[0] = jnp.concatenate(pieces, axis=0).astype(BF16)

    cqt = cosqt_ref[...]
    sqt = sinqt_ref[...]
    half = DSA_ROPE // 2
    off = IDX_HEADS * IDX_DIM
    pieces = []
    for hh in range(DSA_HEADS):
        base = off + DSA_HEAD_DIM * hh
        x1 = r[base:base + half]
        x2 = r[base + half:base + DSA_ROPE]
        pieces.append(x1 * cqt - x2 * sqt)
        pieces.append(x2 * cqt + x1 * sqt)
        pieces.append(r[base + DSA_ROPE:base + DSA_HEAD_DIM])
    dqt_ref[0] = jnp.concatenate(pieces, axis=0).astype(BF16)

    off += DSA_HEADS * DSA_HEAD_DIM
    dct = r[off:off + DSA_LATENT]
    ckvt = dct * lax.rsqrt(jnp.mean(dct * dct, axis=0, keepdims=True) + EPS) * kvgc_ref[...]
    extra = jnp.where(lax.broadcasted_iota(I32, (CKVT_ROWS - DSA_LATENT, tm), 0) == 0, 1.0, 0.0)
    ckvt_ref[0] = jnp.concatenate([ckvt, extra], axis=0).astype(BF16)
    off += DSA_LATENT
    iwt_ref[0] = r[off:off + IDX_HEADS] * (IDX_HEADS ** -0.5 * IDX_DIM ** -0.5)


def _rope_angles(positions, r, theta):
    inv = theta ** (-jnp.arange(0, r, 2, dtype=F32) / r)
    return positions.astype(F32)[:, None] * inv[None, :]


def _in_proj(x, positions, norm_mix, w_in, kv_norm, tm):
    B, S, D = x.shape
    offs = np.cumsum((0,) + IN_SPLITS)
    w_rq, w_rk, w_rv, w_rg, w_dq, w_dc, w_dpe, w_dg, w_iq, w_ik, w_iw = (
        w_in[:, offs[i]:offs[i + 1]] for i in range(len(IN_SPLITS)))

    hp, hi = DSA_ROPE // 2, IDX_ROPE // 2
    pk, pi = KPE_LANE + PARTNER_SHIFT, KIDX_LANE + PARTNER_SHIFT
    small = jnp.zeros((D, LANES), F32)
    small = small.at[:, KPE_LANE:KPE_LANE + DSA_ROPE].set(w_dpe)
    small = small.at[:, KIDX_LANE:KIDX_LANE + IDX_DIM].set(w_ik)
    small = small.at[:, pk:pk + hp].set(-w_dpe[:, hp:])
    small = small.at[:, pk + hp:pk + DSA_ROPE].set(w_dpe[:, :hp])
    small = small.at[:, pi:pi + hi].set(-w_ik[:, hi:IDX_ROPE])
    small = small.at[:, pi + hi:pi + IDX_ROPE].set(w_ik[:, :hi])
    wn = jnp.concatenate([w_rq, w_rk, w_rv, w_rg, w_dg, w_dc, small], axis=1).astype(BF16)
    wt = jnp.concatenate([w_iq.T, w_dq.T, w_dc.T, w_iw.T, jnp.zeros((SUBLANES, D), F32)], axis=0).astype(BF16)
    n_nat = wn.shape[1]
    n_t = wt.shape[0]

    ang_r = _rope_angles(positions, RET_DK, RET_THETA)
    cos_r = jnp.tile(jnp.cos(ang_r), (1, 4))
    sin_r = jnp.tile(jnp.concatenate([-jnp.sin(ang_r), jnp.sin(ang_r)], axis=1), (1, 2))
    ang_p = _rope_angles(positions, DSA_ROPE, ROPE_THETA)
    ang_i = _rope_angles(positions, IDX_ROPE, ROPE_THETA)
    cos_s = jnp.concatenate([jnp.ones((S, PARTNER_SHIFT), F32), jnp.zeros((S, LANES - PARTNER_SHIFT), F32)], axis=1)
    cos_s = cos_s.at[:, KPE_LANE:KPE_LANE + DSA_ROPE].set(jnp.tile(jnp.cos(ang_p), (1, 2)))
    cos_s = cos_s.at[:, KIDX_LANE:KIDX_LANE + IDX_ROPE].set(jnp.tile(jnp.cos(ang_i), (1, 2)))
    sin_s = jnp.zeros((S, LANES), F32)
    sin_s = sin_s.at[:, KPE_LANE:KPE_LANE + DSA_ROPE].set(jnp.tile(jnp.sin(ang_p), (1, 2)))
    sin_s = sin_s.at[:, KIDX_LANE:KIDX_LANE + IDX_ROPE].set(jnp.tile(jnp.sin(ang_i), (1, 2)))
    cos_it = jnp.tile(jnp.cos(ang_i).T, (2, 1))
    sin_it = jnp.concatenate([-jnp.sin(ang_i).T, jnp.sin(ang_i).T], axis=0)
    cos_qt = jnp.cos(ang_p).T
    sin_qt = jnp.sin(ang_p).T

    rw = RET_HEADS * RET_DK
    row = lambda b, s: (b, s, 0)
    col = lambda b, s: (b, 0, s)
    full2 = lambda b, s: (0, 0)
    tok_tab = pl.BlockSpec((tm, LANES), lambda b, s: (s, 0))
    col_tab = pl.BlockSpec((SUBLANES, tm), lambda b, s: (0, s))
    out_shape = (
        jax.ShapeDtypeStruct((B, S, rw), BF16),
        jax.ShapeDtypeStruct((B, S, rw), BF16),
        jax.ShapeDtypeStruct((B, S, rw), BF16),
        jax.ShapeDtypeStruct((B, S, rw), BF16),
        jax.ShapeDtypeStruct((B, S, rw), BF16),
        jax.ShapeDtypeStruct((B, S, KEY_W), BF16),
        jax.ShapeDtypeStruct((B, CKVT_ROWS, S), BF16),
        jax.ShapeDtypeStruct((B, IDX_HEADS * IDX_DIM, S), BF16),
        jax.ShapeDtypeStruct((B, DSA_HEADS * DSA_HEAD_DIM, S), BF16),
        jax.ShapeDtypeStruct((B, IDX_HEADS, S), F32),
    )
    out_specs = (
        pl.BlockSpec((1, tm, rw), row), pl.BlockSpec((1, tm, rw), row), pl.BlockSpec((1, tm, rw), row),
        pl.BlockSpec((1, tm, rw), row), pl.BlockSpec((1, tm, rw), row),
        pl.BlockSpec((1, tm, KEY_W), row),
        pl.BlockSpec((1, CKVT_ROWS, tm), col),
        pl.BlockSpec((1, IDX_HEADS * IDX_DIM, tm), col),
        pl.BlockSpec((1, DSA_HEADS * DSA_HEAD_DIM, tm), col),
        pl.BlockSpec((1, IDX_HEADS, tm), col),
    )
    in_specs = [
        pl.BlockSpec((1, tm, D), row),
        pl.BlockSpec((1, D), full2),
        pl.BlockSpec((D, n_nat), full2),
        pl.BlockSpec((n_t, D), full2),
        pl.BlockSpec((1, DSA_LATENT), full2),
        pl.BlockSpec((DSA_LATENT, 1), full2),
        tok_tab, tok_tab, tok_tab, tok_tab, col_tab, col_tab, col_tab, col_tab,
    ]
    return pl.pallas_call(
        _in_proj_kernel,
        grid=(B, S // tm),
        in_specs=in_specs,
        out_specs=out_specs,
        out_shape=out_shape,
        compiler_params=pltpu.CompilerParams(
            dimension_semantics=("arbitrary", "arbitrary"), vmem_limit_bytes=VMEM_LIMIT_BYTES),
        name="in_proj",
    )(x, norm_mix.reshape(1, D), wn, wt, kv_norm.reshape(1, DSA_LATENT), kv_norm.reshape(DSA_LATENT, 1),
      cos_r, sin_r, cos_s, sin_s, cos_it, sin_it, cos_qt, sin_qt)


def _retention_kernel(q_ref, k_ref, v_ref, g_ref, dmat_ref, xi_ref, zeta_ref, gl_ref, nrm_ref,
                      o_ref, state_ref, *, n_blocks, blk):
    lane = lax.broadcasted_iota(I32, (blk, LANES), 1)
    lo = lane < RET_DK
    r2 = lax.broadcasted_iota(I32, (LANES, LANES), 0) < RET_DK
    c2 = lax.broadcasted_iota(I32, (LANES, LANES), 1) < RET_DK
    same_head = r2 == c2
    npair = state_ref.shape[0]
    state_ref[...] = jnp.zeros(state_ref.shape, F32)
    inv_dv = 1.0 / RET_DK

    def one_block(r0):
        for p in range(npair):
            cols = slice(LANES * p, LANES * (p + 1))
            q = q_ref[0, pl.ds(r0, blk), cols]
            k = k_ref[0, pl.ds(r0, blk), cols]
            v = v_ref[0, pl.ds(r0, blk), cols]
            qf = q.astype(F32)
            q0 = jnp.where(lo, qf, 0.0).astype(BF16)
            q1 = jnp.where(lo, 0.0, qf).astype(BF16)
            p0 = (_dot_nt(q0, k) * dmat_ref[p, 0]).astype(BF16)
            p1 = (_dot_nt(q1, k) * dmat_ref[p, 1]).astype(BF16)
            o_intra = jnp.where(lo, _dot(p0, v), _dot(p1, v))
            st = state_ref[p]
            o_cross = _dot((qf * xi_ref[p]).astype(BF16), st.astype(BF16))
            o = o_intra + o_cross
            kzt = (k.astype(F32) * zeta_ref[p]).T.astype(BF16)
            state_ref[p] = st * gl_ref[p] + jnp.where(same_head, _dot(kzt, v), 0.0)

            s_lo = jnp.sum(jnp.where(lo, o, 0.0), axis=-1, keepdims=True)
            s_hi = jnp.sum(jnp.where(lo, 0.0, o), axis=-1, keepdims=True)
            d = o - jnp.where(lo, s_lo, s_hi) * inv_dv
            d2 = d * d
            v_lo = jnp.sum(jnp.where(lo, d2, 0.0), axis=-1, keepdims=True)
            v_hi = jnp.sum(jnp.where(lo, 0.0, d2), axis=-1, keepdims=True)
            var = jnp.where(lo, v_lo, v_hi) * inv_dv
            y = d * lax.rsqrt(var + EPS) * nrm_ref[p] * g_ref[0, pl.ds(r0, blk), cols].astype(F32)
            o_ref[0, pl.ds(r0, blk), cols] = y.astype(BF16)

    per_trip = next(u for u in (4, 2, 1) if n_blocks % u == 0)

    def body(i, carry):
        for u in range(per_trip):
            one_block(pl.multiple_of((i * per_trip + u) * blk, blk))
        return carry

    lax.fori_loop(0, n_blocks // per_trip, body, 0)


def _retention_tables(blk):
    hs = np.arange(RET_HEADS, dtype=np.float64)
    log_g = np.log1p(-np.exp2(-5.0 - hs))
    t = np.arange(blk, dtype=np.float64)
    dist = np.abs(t[:, None] - t[None, :])
    causal = (t[None, :] // CHUNK) <= (t[:, None] // CHUNK)
    dmat = np.where(causal[None], np.exp(log_g[:, None, None] * dist[None]), 0.0)
    xi = np.exp(log_g[:, None] * (t + 1.0)[None, :])
    zeta = np.exp(log_g[:, None] * (blk - 1.0 - t)[None, :])
    gl = np.exp(log_g * blk)
    npair = RET_HEADS // 2

    def lanes(a):
        a = a.reshape(npair, 2, blk)
        return np.repeat(np.transpose(a, (0, 2, 1)), RET_DK, axis=2)

    gl_l = np.repeat(gl.reshape(npair, 1, 2), RET_DK, axis=2)
    f = lambda a: jnp.asarray(a, F32)
    return f(dmat.reshape(npair, 2, blk, blk)), f(lanes(xi)), f(lanes(zeta)), f(gl_l)


def _retention(rq, rk, rv, rg, ret_norm, blk):
    B, S, W = rq.shape
    npair = W // LANES
    dmat, xi, zeta, gl = _retention_tables(blk)
    seq = pl.BlockSpec((1, S, W), lambda b: (b, 0, 0))
    tab = pl.BlockSpec((npair, blk, LANES), lambda b: (0, 0, 0))
    vec = pl.BlockSpec((npair, 1, LANES), lambda b: (0, 0, 0))
    return pl.pallas_call(
        functools.partial(_retention_kernel, n_blocks=S // blk, blk=blk),
        grid=(B,),
        in_specs=[seq, seq, seq, seq,
                  pl.BlockSpec((npair, 2, blk, blk), lambda b: (0, 0, 0, 0)),
                  tab, tab, vec, vec],
        out_specs=seq,
        out_shape=jax.ShapeDtypeStruct((B, S, W), BF16),
        scratch_shapes=[pltpu.VMEM((npair, LANES, LANES), F32)],
        compiler_params=pltpu.CompilerParams(
            dimension_semantics=("arbitrary",), vmem_limit_bytes=VMEM_LIMIT_BYTES),
        name="retention",
    )(rq, rk, rv, rg, dmat, xi, zeta, gl, ret_norm.reshape(npair, 1, LANES))


def _for_tile_groups(n_tiles, body):
    n4 = n_tiles >> 2

    def loop_body(c, carry):
        body(c * 4, 4)
        return carry

    lax.fori_loop(0, n4, loop_body, 0)

    @pl.when((n_tiles & 2) != 0)
    def _():
        body(n4 * 4, 2)

    @pl.when((n_tiles & 1) != 0)
    def _():
        body(n4 * 4 + (n_tiles & 2), 1)


def _dsa_kernel(keys_ref, ckvt_ref, iqt_ref, dqt_ref, iwt_ref, wukt_ref, wuvt_ref, gate_ref,
                o_ref, qit_ref, qt_ref, sc_ref, kc_ref, lg_ref, acc_ref, thr_ref, cnt_ref, mx_ref,
                *, n_sel, n_qblk):
    j = pl.program_id(1)
    T = DSA_QB
    HT = DSA_HEADS * T
    n_tiles = j + 1
    scale = DSA_HEAD_DIM ** -0.5 * LOG2E

    @pl.when(j == 0)
    def _():
        qit_ref[...] = jnp.zeros(qit_ref.shape, BF16)
        qt_ref[...] = jnp.zeros(qt_ref.shape, BF16)

    for h in range(DSA_HEADS):
        cols = slice(T * h, T * (h + 1))
        r0 = LANES + KIDX_LANE
        qit_ref[r0:r0 + IDX_DIM, cols] = iqt_ref[0, IDX_DIM * h:IDX_DIM * (h + 1), :]
        dq = dqt_ref[0, DSA_HEAD_DIM * h:DSA_HEAD_DIM * (h + 1), :]
        qlat = _dot(wukt_ref[h], dq) * scale
        qt_ref[0:DSA_LATENT, cols] = qlat.astype(BF16)
        r1 = LANES + KPE_LANE
        qt_ref[r1:r1 + DSA_ROPE, cols] = (dq[0:DSA_ROPE].astype(F32) * scale).astype(BF16)

    def key_tile(kt):
        return keys_ref[0, pl.ds(pl.multiple_of(kt * T, T), T), :]

    w = iwt_ref[0]

    def score_tiles(first, g):
        for t in range(g):
            kt = first + t
            s = _dot(key_tile(kt), qit_ref[...])
            score = jnp.zeros((T, T), F32)
            for h in range(IDX_HEADS):
                score = score + jnp.maximum(s[:, T * h:T * (h + 1)], 0.0) * w[h:h + 1, :]
            rows = pl.ds(pl.multiple_of(kt * T, T), T)
            sc_ref[rows, :] = score
            kc_ref[rows, :] = score.astype(BF16)

    _for_tile_groups(n_tiles, score_tiles)
    r_hid = pl.multiple_of(j * T + CHUNK, CHUNK)
    key_chunk = (lax.broadcasted_iota(I32, (T - CHUNK, T), 0) >> CHUNK_SHIFT) + 1
    query_chunk = lax.broadcasted_iota(I32, (T - CHUNK, T), 1) >> CHUNK_SHIFT
    masked = jnp.where(key_chunk > query_chunk, -jnp.inf, sc_ref[pl.ds(r_hid, T - CHUNK), :])
    sc_ref[pl.ds(r_hid, T - CHUNK), :] = masked
    kc_ref[pl.ds(r_hid, T - CHUNK), :] = masked.astype(BF16)

    trows = T // SUBLANES
    full = lambda v: jnp.full((SUBLANES, T), v, I32)
    one_c = jnp.ones((PACKED_SUBLANES, T), BF16)
    zero_c = jnp.zeros((PACKED_SUBLANES, T), BF16)
    sub_total = lambda acc: jnp.broadcast_to(jnp.sum(acc, axis=0, keepdims=True), (SUBLANES, T))
    key_to_float = lambda key: lax.bitcast_convert_type(key ^ ((key >> 31) & INT_MAX), F32)
    thr_ref[...] = jnp.full((SUBLANES, T), F32_LOWEST, F32)
    cnt_ref[...] = full(n_sel)

    def descend(nt):
        def update(c, cand, cnt):
            thr, cnt_thr = c
            take = cnt >= n_sel
            return jnp.where(take, cand, thr), jnp.where(take, cnt, cnt_thr)

        def coarse_step(i, c):
            cand = c[0] + lax.shift_left(jnp.int32(1), 31 - i)
            hi = cand >> 16
            grid = lax.bitcast_convert_type((hi ^ ((hi >> 31) & 0x7FFF)) << 16, F32).astype(BF16)
            grid = jnp.broadcast_to(grid[0:1, :], (PACKED_SUBLANES, T))
            accs = [zero_c] * 4
            for idx in range(nt * T // PACKED_SUBLANES):
                row = PACKED_SUBLANES * idx
                accs[idx % 4] = accs[idx % 4] + jnp.where(kc_ref[row:row + PACKED_SUBLANES, :] >= grid, one_c, zero_c)
            total = (accs[0].astype(F32) + accs[1].astype(F32)) + (accs[2].astype(F32) + accs[3].astype(F32))
            cnt = jnp.sum(total, axis=0, keepdims=True).astype(I32)
            return update(c, cand, jnp.broadcast_to(cnt, (SUBLANES, T)))

        def fine_step(i, c):
            cand = c[0] + lax.shift_left(jnp.int32(1), 31 - i)
            cand_f = key_to_float(cand)
            accs = [full(0)] * 4
            for idx in range(nt * trows):
                row = SUBLANES * idx
                accs[idx % 4] = accs[idx % 4] + jnp.where(sc_ref[row:row + SUBLANES, :] >= cand_f, 1, 0)
            return update(c, cand, sub_total((accs[0] + accs[1]) + (accs[2] + accs[3])))

        start = (full(INT_MIN), full(nt * T))
        coarse, _ = lax.fori_loop(0, COARSE_BITS, coarse_step, start)
        base = jnp.where(coarse <= NEG_INF_KEY, INT_MIN, coarse - (1 << (32 - COARSE_BITS)))
        thr, cnt_thr = lax.fori_loop(COARSE_BITS - 2, 32, fine_step, (base, full(nt * T)))
        few = thr <= NEG_INF_KEY
        thr_ref[...] = jnp.where(few, F32_LOWEST, key_to_float(thr))
        cnt_ref[...] = jnp.where(few, n_sel, cnt_thr)

    for nt in range(n_sel // T + 1, n_qblk + 1):
        pl.when(n_tiles == nt)(functools.partial(descend, nt))

    @pl.when(jnp.max(jnp.where(cnt_ref[0:1, :] > n_sel, 1, 0)) > 0)
    def _demote_excess_ties():
        thr = thr_ref[...]
        excess = cnt_ref[...] - n_sel
        sub = lax.broadcasted_iota(I32, (SUBLANES, T), 0)

        def tile_body(i, later):
            rows = pl.ds(pl.multiple_of((n_tiles - 1 - i) * T, T), T)
            tile = sc_ref[rows, :]
            kept = [None] * trows
            for v in range(trows - 1, -1, -1):
                sc = tile[SUBLANES * v:SUBLANES * (v + 1), :]
                tied = jnp.where(sc == thr, 1, 0)
                after = tied
                for k in (1, 2, 4):
                    after = after + jnp.where(sub < SUBLANES - k, pltpu.roll(after, SUBLANES - k, 0), 0)
                rank_from_end = later + after - tied
                drop = jnp.where(tied > 0, jnp.where(rank_from_end < excess, 1, 0), 0)
                kept[v] = jnp.where(drop > 0, -jnp.inf, sc)
                later = later + jnp.broadcast_to(after[0:1, :], (SUBLANES, T))
            sc_ref[rows, :] = jnp.concatenate(kept, axis=0)
            return later

        lax.fori_loop(0, n_tiles, tile_body, full(0))

    thr_f = thr_ref[...]

    mx_ref[...] = jnp.full((SUBLANES, HT), -jnp.inf, F32)

    def logit_tiles(first, g):
        mx = mx_ref[...]
        for t in range(g):
            kt = first + t
            r0 = pl.multiple_of(kt * T, T)
            l = _dot(key_tile(kt), qt_ref[...])
            sc = sc_ref[pl.ds(r0, T), :].reshape(trows, SUBLANES, T)
            bias = jnp.where(sc >= thr_f[None], 0.0, -jnp.inf).reshape(T, T)
            new = []
            for h in range(DSA_HEADS):
                lm = l[:, T * h:T * (h + 1)] + bias
                lg_ref[pl.ds(r0, T), T * h:T * (h + 1)] = lm
                new.append(jnp.max(lm.reshape(trows, SUBLANES, T), axis=0))
            mx = jnp.maximum(mx, jnp.concatenate(new, axis=1))
        mx_ref[...] = mx

    _for_tile_groups(n_tiles, logit_tiles)
    mrow = jnp.broadcast_to(jnp.max(mx_ref[...], axis=0, keepdims=True), (SUBLANES, HT))

    acc_ref[...] = jnp.zeros(acc_ref.shape, F32)

    def pv_tiles(first, g):
        r0 = pl.multiple_of(first * T, T)
        ps = []
        for t in range(g):
            lm = lg_ref[pl.ds(pl.multiple_of((first + t) * T, T), T), :]
            p = jnp.exp2(lm.reshape(trows, SUBLANES, HT) - mrow[None])
            ps.append(p.reshape(T, HT).astype(BF16))
        p_all = ps[0] if g == 1 else jnp.concatenate(ps, axis=0)
        acc_ref[...] += _dot(ckvt_ref[0, :, pl.ds(r0, g * T)], p_all)

    _for_tile_groups(n_tiles, pv_tiles)
    linv = 1.0 / acc_ref[DSA_LATENT:DSA_LATENT + 1, :]
    olat = (acc_ref[0:DSA_LATENT, :] * linv).astype(BF16)
    outs = [_dot(wuvt_ref[h], olat[:, T * h:T * (h + 1)]) for h in range(DSA_HEADS)]
    o = jnp.concatenate(outs, axis=0).T
    o_ref[0] = (o * gate_ref[0].astype(F32)).astype(BF16)


def _dsa(keys, ckvt, iqt, dqt, iwt, w_uk, w_uv, dgate):
    B, S, _ = keys.shape
    T = DSA_QB
    assert S % T == 0 and T % CHUNK == 0
    n_qblk = S // T
    n_sel = min(TOPK_MAX, S // 4)
    H = DSA_HEADS
    wukt = jnp.concatenate([jnp.zeros((H, DSA_LATENT, DSA_ROPE), F32), jnp.swapaxes(w_uk, 1, 2)], axis=2).astype(BF16)
    wuvt = jnp.swapaxes(w_uv, 1, 2).astype(BF16)
    HT = H * T
    per_b = lambda b, j: (b, 0, 0)
    qcol = lambda b, j: (b, 0, j)
    full3 = lambda b, j: (0, 0, 0)
    return pl.pallas_call(
        functools.partial(_dsa_kernel, n_sel=n_sel, n_qblk=n_qblk),
        grid=(B, n_qblk),
        in_specs=[
            pl.BlockSpec((1, S, KEY_W), per_b),
            pl.BlockSpec((1, CKVT_ROWS, S), per_b),
            pl.BlockSpec((1, IDX_HEADS * IDX_DIM, T), qcol),
            pl.BlockSpec((1, H * DSA_HEAD_DIM, T), qcol),
            pl.BlockSpec((1, IDX_HEADS, T), qcol),
            pl.BlockSpec((H, DSA_LATENT, DSA_HEAD_DIM), full3),
            pl.BlockSpec((H, DSA_VDIM, DSA_LATENT), full3),
            pl.BlockSpec((1, T, H * DSA_VDIM), lambda b, j: (b, j, 0)),
        ],
        out_specs=pl.BlockSpec((1, T, H * DSA_VDIM), lambda b, j: (b, j, 0)),
        out_shape=jax.ShapeDtypeStruct((B, S, H * DSA_VDIM), BF16),
        scratch_shapes=[
            pltpu.VMEM((KEY_W, HT), BF16),
            pltpu.VMEM((KEY_W, HT), BF16),
            pltpu.VMEM((S, T), F32),
            pltpu.VMEM((S, T), BF16),
            pltpu.VMEM((S, HT), F32),
            pltpu.VMEM((CKVT_ROWS, HT), F32),
            pltpu.VMEM((SUBLANES, T), F32),
            pltpu.VMEM((SUBLANES, T), I32),
            pltpu.VMEM((SUBLANES, HT), F32),
        ],
        compiler_params=pltpu.CompilerParams(
            dimension_semantics=("arbitrary", "arbitrary"), vmem_limit_bytes=VMEM_LIMIT_BYTES),
        name="dsa",
    )(keys, ckvt, iqt, dqt, iwt, wukt, wuvt, dgate)


def _out_kernel(x_ref, ret_ref, dsa_ref, p_ref, wo_ref, gp_ref, wg_ref, wp_ref, gf_ref, o_ref, *, final_norm):
    nr = ret_ref.shape[2]
    x1 = x_ref[0] + _dot(ret_ref[0], wo_ref[0:nr, :]) + _dot(dsa_ref[0], wo_ref[nr:, :])
    ms = jnp.mean(x1 * x1, axis=-1, keepdims=True)
    h = (x1 * lax.rsqrt(ms + EPS) * gp_ref[...]).astype(BF16)
    z = _dot(h, wg_ref[...])
    gate = 1.0 / (1.0 + jnp.exp(-z))
    x2 = x1 + gate * _dot(p_ref[0].astype(BF16), wp_ref[...])
    if final_norm:
        ms2 = jnp.mean(x2 * x2, axis=-1, keepdims=True)
        x2 = x2 * lax.rsqrt(ms2 + EPS) * gf_ref[...]
    o_ref[0] = x2


def _out_proj(x, ret, dsa, p, w_out, norm_ple, w_gate, w_proj, norm_final, tm, final_norm):
    B, S, D = x.shape
    row = lambda b, s: (b, s, 0)
    full2 = lambda b, s: (0, 0)
    return pl.pallas_call(
        functools.partial(_out_kernel, final_norm=final_norm),
        grid=(B, S // tm),
        in_specs=[
            pl.BlockSpec((1, tm, D), row),
            pl.BlockSpec((1, tm, ret.shape[2]), row),
            pl.BlockSpec((1, tm, dsa.shape[2]), row),
            pl.BlockSpec((1, tm, p.shape[2]), row),
            pl.BlockSpec(w_out.shape, full2),
            pl.BlockSpec((1, D), full2),
            pl.BlockSpec(w_gate.shape, full2),
            pl.BlockSpec(w_proj.shape, full2),
            pl.BlockSpec((1, D), full2),
        ],
        out_specs=pl.BlockSpec((1, tm, D), row),
        out_shape=jax.ShapeDtypeStruct((B, S, D), F32),
        compiler_params=pltpu.CompilerParams(
            dimension_semantics=("arbitrary", "arbitrary"), vmem_limit_bytes=VMEM_LIMIT_BYTES),
        name="out_proj",
    )(x, ret, dsa, p, w_out.astype(BF16), norm_ple.reshape(1, D), w_gate.astype(BF16),
      w_proj.astype(BF16), norm_final.reshape(1, D))


def _tiles(S):
    tm = math.gcd(S, 512)
    blk = math.gcd(S, 256)
    return tm, blk


def kernel(x, p, positions, norm_mix, w_in, ret_norm, kv_norm, w_uk, w_uv, w_out, norm_ple, w_ple_gate, w_ple_proj, norm_final):
    B, S, D = x.shape
    depth = w_in.shape[0]
    tm, blk = _tiles(S)
    assert blk % CHUNK == 0 and (1 << CHUNK_SHIFT) == CHUNK
    for i in range(depth):
        rq, rk, rv, rg, dg, keys, ckvt, iqt, dqt, iwt = _in_proj(x, positions, norm_mix[i], w_in[i], kv_norm[i], tm)
        ret = _retention(rq, rk, rv, rg, ret_norm[i], blk)
        dsa = _dsa(keys, ckvt, iqt, dqt, iwt, w_uk[i], w_uv[i], dg)
        x = _out_proj(x, ret, dsa, p[i], w_out[i], norm_ple[i], w_ple_gate[i], w_ple_proj[i],
                      norm_final, tm, final_norm=(i == depth - 1))
    return x
```

```python
import functools
import math

import numpy as np
import jax
import jax.numpy as jnp
from jax import lax
from jax.experimental import pallas as pl
from jax.experimental.pallas import tpu as pltpu

F32 = jnp.float32
BF16 = jnp.bfloat16
I32 = jnp.int32

LANES = 128
SUBLANES = 8
VMEM_LIMIT_BYTES = 48 * 1024 * 1024

CHUNK = 64
CHUNK_SHIFT = 6
DSA_QB = 256
EPS = 1e-6
ROPE_THETA = 500000.0
RET_HEADS = 8
RET_DK = 64
RET_THETA = 10000.0
DSA_HEADS = 8
DSA_HEAD_DIM = 64
DSA_ROPE = 16
DSA_LATENT = 128
DSA_VDIM = 64
TOPK_MAX = 256
IDX_HEADS = 8
IDX_DIM = 32
IDX_ROPE = 8
IN_SPLITS = (512, 512, 512, 512, 512, 128, 16, 512, 256, 32, 8)

KPE_LANE = 0
KIDX_LANE = 32
PARTNER_SHIFT = 64
KEY_W = 2 * LANES
CKVT_ROWS = DSA_LATENT + 16

INT_MIN = int(np.iinfo(np.int32).min)
INT_MAX = int(np.iinfo(np.int32).max)
NEG_INF_KEY = INT_MIN + 0x7FFFFF
F32_LOWEST = float(np.finfo(np.float32).min)

NT_DIMS = (((1,), (1,)), ((), ()))
LOG2E = math.log2(math.e)
PACKED_SUBLANES = 16
COARSE_BITS = 16


def _dot(a, b):
    return jnp.dot(a, b, preferred_element_type=F32)


def _dot_nt(a, b):
    return lax.dot_general(a, b, NT_DIMS, preferred_element_type=F32)


def _silu(y):
    return y / (1.0 + jnp.exp(-y))


def _in_proj_kernel(x_ref, g_ref, wn_ref, wt_ref, kvg_ref, kvgc_ref,
                    cosr_ref, sinr_ref, coss_ref, sins_ref,
                    cosit_ref, sinit_ref, cosqt_ref, sinqt_ref,
                    rq_ref, rk_ref, rv_ref, rg_ref, dg_ref, keys_ref,
                    ckvt_ref, iqt_ref, dqt_ref, iwt_ref):
    x = x_ref[0]
    tm = x.shape[0]
    ms = jnp.mean(x * x, axis=-1, keepdims=True)
    h = (x * lax.rsqrt(ms + EPS) * g_ref[...]).astype(BF16)

    def nat(c0, width):
        return _dot(h, wn_ref[:, c0:c0 + width])

    cosr = cosr_ref[...]
    sinr = sinr_ref[...]
    lane = lax.broadcasted_iota(I32, (tm, LANES), 1)
    first_half = (lane & (RET_DK // 2)) == 0

    def rope64(y):
        outs = []
        for g in range(y.shape[1] // LANES):
            yg = y[:, LANES * g:LANES * (g + 1)]
            partner = jnp.where(first_half, pltpu.roll(yg, LANES - RET_DK // 2, 1),
                                pltpu.roll(yg, RET_DK // 2, 1))
            outs.append(yg * cosr + partner * sinr)
        return jnp.concatenate(outs, axis=1)

    rw = RET_HEADS * RET_DK
    rq_ref[0] = (rope64(nat(0, rw)) * (RET_DK ** -0.5)).astype(BF16)
    rk_ref[0] = rope64(nat(rw, rw)).astype(BF16)
    rv_ref[0] = nat(2 * rw, rw).astype(BF16)
    rg_ref[0] = _silu(nat(3 * rw, rw)).astype(BF16)
    dg_ref[0] = _silu(nat(4 * rw, rw)).astype(BF16)
    c0 = 5 * rw
    dc = nat(c0, DSA_LATENT)
    ckv = dc * lax.rsqrt(jnp.mean(dc * dc, axis=-1, keepdims=True) + EPS) * kvg_ref[...]
    keys_ref[0, :, 0:LANES] = ckv.astype(BF16)
    small = nat(c0 + LANES, LANES)
    small = small * coss_ref[...] + pltpu.roll(small, LANES - PARTNER_SHIFT, 1) * sins_ref[...]
    keys_ref[0, :, LANES:KEY_W] = small.astype(BF16)

    r = _dot_nt(wt_ref[...], h)
    cit = cosit_ref[...]
    sit = sinit_ref[...]
    pieces = []
    for hh in range(IDX_HEADS):
        base = IDX_DIM * hh
        xr = r[base:base + IDX_ROPE]
        pieces.append(xr * cit + pltpu.roll(xr, IDX_ROPE // 2, 0) * sit)
        pieces.append(r[base + IDX_ROPE:base + IDX_DIM])
    iqt_ref[0] = jnp.concatenate(pieces, axis=0).astype(BF16)

    cqt = cosqt_ref[...]
    sqt = sinqt_ref[...]
    half = DSA_ROPE // 2
    off = IDX_HEADS * IDX_DIM
    pieces = []
    for hh in range(DSA_HEADS):
        base = off + DSA_HEAD_DIM * hh
        x1 = r[base:base + half]
        x2 = r[base + half:base + DSA_ROPE]
        pieces.append(x1 * cqt - x2 * sqt)
        pieces.append(x2 * cqt + x1 * sqt)
        pieces.append(r[base + DSA_ROPE:base + DSA_HEAD_DIM])
    dqt_ref[0] = jnp.concatenate(pieces, axis=0).astype(BF16)

    off += DSA_HEADS * DSA_HEAD_DIM
    dct = r[off:off + DSA_LATENT]
    ckvt = dct * lax.rsqrt(jnp.mean(dct * dct, axis=0, keepdims=True) + EPS) * kvgc_ref[...]
    extra = jnp.where(lax.broadcasted_iota(I32, (CKVT_ROWS - DSA_LATENT, tm), 0) == 0, 1.0, 0.0)
    ckvt_ref[0] = jnp.concatenate([ckvt, extra], axis=0).astype(BF16)
    off += DSA_LATENT
    iwt_ref[0] = r[off:off + IDX_HEADS] * (IDX_HEADS ** -0.5 * IDX_DIM ** -0.5)


def _rope_angles(positions, r, theta):
    inv = theta ** (-jnp.arange(0, r, 2, dtype=F32) / r)
    return positions.astype(F32)[:, None] * inv[None, :]


def _in_proj(x, positions, norm_mix, w_in, kv_norm, tm):
    B, S, D = x.shape
    offs = np.cumsum((0,) + IN_SPLITS)
    w_rq, w_rk, w_rv, w_rg, w_dq, w_dc, w_dpe, w_dg, w_iq, w_ik, w_iw = (
        w_in[:, offs[i]:offs[i + 1]] for i in range(len(IN_SPLITS)))

    hp, hi = DSA_ROPE // 2, IDX_ROPE // 2
    pk, pi = KPE_LANE + PARTNER_SHIFT, KIDX_LANE + PARTNER_SHIFT
    small = jnp.zeros((D, LANES), F32)
    small = small.at[:, KPE_LANE:KPE_LANE + DSA_ROPE].set(w_dpe)
    small = small.at[:, KIDX_LANE:KIDX_LANE + IDX_DIM].set(w_ik)
    small = small.at[:, pk:pk + hp].set(-w_dpe[:, hp:])
    small = small.at[:, pk + hp:pk + DSA_ROPE].set(w_dpe[:, :hp])
    small = small.at[:, pi:pi + hi].set(-w_ik[:, hi:IDX_ROPE])
    small = small.at[:, pi + hi:pi + IDX_ROPE].set(w_ik[:, :hi])
    wn = jnp.concatenate([w_rq, w_rk, w_rv, w_rg, w_dg, w_dc, small], axis=1).astype(BF16)
    wt = jnp.concatenate([w_iq.T, w_dq.T, w_dc.T, w_iw.T, jnp.zeros((SUBLANES, D), F32)], axis=0).astype(BF16)
    n_nat = wn.shape[1]
    n_t = wt.shape[0]

    ang_r = _rope_angles(positions, RET_DK, RET_THETA)
    cos_r = jnp.tile(jnp.cos(ang_r), (1, 4))
    sin_r = jnp.tile(jnp.concatenate([-jnp.sin(ang_r), jnp.sin(ang_r)], axis=1), (1, 2))
    ang_p = _rope_angles(positions, DSA_ROPE, ROPE_THETA)
    ang_i = _rope_angles(positions, IDX_ROPE, ROPE_THETA)
    cos_s = jnp.broadcast_to(jnp.where(jnp.arange(LANES) < PARTNER_SHIFT, 1.0, 0.0).astype(F32), (S, LANES))
    cos_s = cos_s.at[:, KPE_LANE:KPE_LANE + DSA_ROPE].set(jnp.tile(jnp.cos(ang_p), (1, 2)))
    cos_s = cos_s.at[:, KIDX_LANE:KIDX_LANE + IDX_ROPE].set(jnp.tile(jnp.cos(ang_i), (1, 2)))
    sin_s = jnp.zeros((S, LANES), F32)
    sin_s = sin_s.at[:, KPE_LANE:KPE_LANE + DSA_ROPE].set(jnp.tile(jnp.sin(ang_p), (1, 2)))
    sin_s = sin_s.at[:, KIDX_LANE:KIDX_LANE + IDX_ROPE].set(jnp.tile(jnp.sin(ang_i), (1, 2)))
    cos_it = jnp.tile(jnp.cos(ang_i).T, (2, 1))
    sin_it = jnp.concatenate([-jnp.sin(ang_i).T, jnp.sin(ang_i).T], axis=0)
    cos_qt = jnp.cos(ang_p).T
    sin_qt = jnp.sin(ang_p).T

    rw = RET_HEADS * RET_DK
    row = lambda b, s: (b, s, 0)
    col = lambda b, s: (b, 0, s)
    full2 = lambda b, s: (0, 0)
    tok_tab = pl.BlockSpec((tm, LANES), lambda b, s: (s, 0))
    col_tab = pl.BlockSpec((SUBLANES, tm), lambda b, s: (0, s))
    out_shape = (
        jax.ShapeDtypeStruct((B, S, rw), BF16),
        jax.ShapeDtypeStruct((B, S, rw), BF16),
        jax.ShapeDtypeStruct((B, S, rw), BF16),
        jax.ShapeDtypeStruct((B, S, rw), BF16),
        jax.ShapeDtypeStruct((B, S, rw), BF16),
        jax.ShapeDtypeStruct((B, S, KEY_W), BF16),
        jax.ShapeDtypeStruct((B, CKVT_ROWS, S), BF16),
        jax.ShapeDtypeStruct((B, IDX_HEADS * IDX_DIM, S), BF16),
        jax.ShapeDtypeStruct((B, DSA_HEADS * DSA_HEAD_DIM, S), BF16),
        jax.ShapeDtypeStruct((B, IDX_HEADS, S), F32),
    )
    out_specs = (
        pl.BlockSpec((1, tm, rw), row), pl.BlockSpec((1, tm, rw), row), pl.BlockSpec((1, tm, rw), row),
        pl.BlockSpec((1, tm, rw), row), pl.BlockSpec((1, tm, rw), row),
        pl.BlockSpec((1, tm, KEY_W), row),
        pl.BlockSpec((1, CKVT_ROWS, tm), col),
        pl.BlockSpec((1, IDX_HEADS * IDX_DIM, tm), col),
        pl.BlockSpec((1, DSA_HEADS * DSA_HEAD_DIM, tm), col),
        pl.BlockSpec((1, IDX_HEADS, tm), col),
    )
    in_specs = [
        pl.BlockSpec((1, tm, D), row),
        pl.BlockSpec((1, D), full2),
        pl.BlockSpec((D, n_nat), full2),
        pl.BlockSpec((n_t, D), full2),
        pl.BlockSpec((1, DSA_LATENT), full2),
        pl.BlockSpec((DSA_LATENT, 1), full2),
        tok_tab, tok_tab, tok_tab, tok_tab, col_tab, col_tab, col_tab, col_tab,
    ]
    return pl.pallas_call(
        _in_proj_kernel,
        grid=(B, S // tm),
        in_specs=in_specs,
        out_specs=out_specs,
        out_shape=out_shape,
        compiler_params=pltpu.CompilerParams(
            dimension_semantics=("arbitrary", "arbitrary"), vmem_limit_bytes=VMEM_LIMIT_BYTES),
        name="in_proj",
    )(x, norm_mix.reshape(1, D), wn, wt, kv_norm.reshape(1, DSA_LATENT), kv_norm.reshape(DSA_LATENT, 1),
      cos_r, sin_r, cos_s, sin_s, cos_it, sin_it, cos_qt, sin_qt)


def _retention_kernel(q_ref, k_ref, v_ref, g_ref, dmat_ref, xi_ref, zeta_ref, gl_ref, nrm_ref,
                      o_ref, state_ref, *, n_blocks, blk):
    lane = lax.broadcasted_iota(I32, (blk, LANES), 1)
    lo = lane < RET_DK
    r2 = lax.broadcasted_iota(I32, (LANES, LANES), 0) < RET_DK
    c2 = lax.broadcasted_iota(I32, (LANES, LANES), 1) < RET_DK
    same_head = r2 == c2
    npair = state_ref.shape[0]
    state_ref[...] = jnp.zeros(state_ref.shape, F32)
    inv_dv = 1.0 / RET_DK

    def one_block(r0):
        for p in range(npair):
            cols = slice(LANES * p, LANES * (p + 1))
            q = q_ref[0, pl.ds(r0, blk), cols]
            k = k_ref[0, pl.ds(r0, blk), cols]
            v = v_ref[0, pl.ds(r0, blk), cols]
            qf = q.astype(F32)
            q0 = jnp.where(lo, qf, 0.0).astype(BF16)
            q1 = jnp.where(lo, 0.0, qf).astype(BF16)
            p0 = (_dot_nt(q0, k) * dmat_ref[p, 0]).astype(BF16)
            p1 = (_dot_nt(q1, k) * dmat_ref[p, 1]).astype(BF16)
            o_intra = jnp.where(lo, _dot(p0, v), _dot(p1, v))
            st = state_ref[p]
            o_cross = _dot((qf * xi_ref[p]).astype(BF16), st.astype(BF16))
            o = o_intra + o_cross
            kzt = (k.astype(F32) * zeta_ref[p]).T.astype(BF16)
            state_ref[p] = st * gl_ref[p] + jnp.where(same_head, _dot(kzt, v), 0.0)

            s_lo = jnp.sum(jnp.where(lo, o, 0.0), axis=-1, keepdims=True)
            s_hi = jnp.sum(jnp.where(lo, 0.0, o), axis=-1, keepdims=True)
            d = o - jnp.where(lo, s_lo, s_hi) * inv_dv
            d2 = d * d
            v_lo = jnp.sum(jnp.where(lo, d2, 0.0), axis=-1, keepdims=True)
            v_hi = jnp.sum(jnp.where(lo, 0.0, d2), axis=-1, keepdims=True)
            var = jnp.where(lo, v_lo, v_hi) * inv_dv
            y = d * lax.rsqrt(var + EPS) * nrm_ref[p] * g_ref[0, pl.ds(r0, blk), cols].astype(F32)
            o_ref[0, pl.ds(r0, blk), cols] = y.astype(BF16)

    per_trip = next(u for u in (8, 4, 2, 1) if n_blocks % u == 0)

    def body(i, carry):
        for u in range(per_trip):
            one_block(pl.multiple_of((i * per_trip + u) * blk, blk))
        return carry

    lax.fori_loop(0, n_blocks // per_trip, body, 0)


def _retention_tables(blk):
    hs = np.arange(RET_HEADS, dtype=np.float64)
    log_g = np.log1p(-np.exp2(-5.0 - hs))
    t = np.arange(blk, dtype=np.float64)
    dist = np.abs(t[:, None] - t[None, :])
    causal = (t[None, :] // CHUNK) <= (t[:, None] // CHUNK)
    dmat = np.where(causal[None], np.exp(log_g[:, None, None] * dist[None]), 0.0)
    xi = np.exp(log_g[:, None] * (t + 1.0)[None, :])
    zeta = np.exp(log_g[:, None] * (blk - 1.0 - t)[None, :])
    gl = np.exp(log_g * blk)
    npair = RET_HEADS // 2

    def lanes(a):
        a = a.reshape(npair, 2, blk)
        return np.repeat(np.transpose(a, (0, 2, 1)), RET_DK, axis=2)

    gl_l = np.repeat(gl.reshape(npair, 1, 2), RET_DK, axis=2)
    f = lambda a: jnp.asarray(a, F32)
    return f(dmat.reshape(npair, 2, blk, blk)), f(lanes(xi)), f(lanes(zeta)), f(gl_l)


def _retention(rq, rk, rv, rg, ret_norm, blk):
    B, S, W = rq.shape
    npair = W // LANES
    dmat, xi, zeta, gl = _retention_tables(blk)
    seq = pl.BlockSpec((1, S, W), lambda b: (b, 0, 0))
    tab = pl.BlockSpec((npair, blk, LANES), lambda b: (0, 0, 0))
    vec = pl.BlockSpec((npair, 1, LANES), lambda b: (0, 0, 0))
    return pl.pallas_call(
        functools.partial(_retention_kernel, n_blocks=S // blk, blk=blk),
        grid=(B,),
        in_specs=[seq, seq, seq, seq,
                  pl.BlockSpec((npair, 2, blk, blk), lambda b: (0, 0, 0, 0)),
                  tab, tab, vec, vec],
        out_specs=seq,
        out_shape=jax.ShapeDtypeStruct((B, S, W), BF16),
        scratch_shapes=[pltpu.VMEM((npair, LANES, LANES), F32)],
        compiler_params=pltpu.CompilerParams(
            dimension_semantics=("arbitrary",), vmem_limit_bytes=VMEM_LIMIT_BYTES),
        name="retention",
    )(rq, rk, rv, rg, dmat, xi, zeta, gl, ret_norm.reshape(npair, 1, LANES))


def _for_tile_groups(n_tiles, body):
    n4 = n_tiles >> 2

    def loop_body(c, carry):
        body(c * 4, 4)
        return carry

    lax.fori_loop(0, n4, loop_body, 0)

    @pl.when((n_tiles & 2) != 0)
    def _():
        body(n4 * 4, 2)

    @pl.when((n_tiles & 1) != 0)
    def _():
        body(n4 * 4 + (n_tiles & 2), 1)


def _dsa_kernel(keys_ref, ckvt_ref, iqt_ref, dqt_ref, iwt_ref, wukt_ref, wuvt_ref, gate_ref,
                o_ref, qit_ref, qt_ref, sc_ref, kc_ref, lg_ref, acc_ref, thr_ref, cnt_ref, mx_ref,
                *, n_sel, n_qblk):
    j = pl.program_id(1)
    T = DSA_QB
    HT = DSA_HEADS * T
    n_tiles = j + 1
    scale = DSA_HEAD_DIM ** -0.5 * LOG2E

    @pl.when(j == 0)
    def _():
        qit_ref[...] = jnp.zeros(qit_ref.shape, BF16)
        qt_ref[...] = jnp.zeros(qt_ref.shape, BF16)

    for h in range(DSA_HEADS):
        cols = slice(T * h, T * (h + 1))
        r0 = LANES + KIDX_LANE
        qit_ref[r0:r0 + IDX_DIM, cols] = iqt_ref[0, IDX_DIM * h:IDX_DIM * (h + 1), :]
        dq = dqt_ref[0, DSA_HEAD_DIM * h:DSA_HEAD_DIM * (h + 1), :]
        qlat = _dot(wukt_ref[h], dq) * scale
        qt_ref[0:DSA_LATENT, cols] = qlat.astype(BF16)
        r1 = LANES + KPE_LANE
        qt_ref[r1:r1 + DSA_ROPE, cols] = (dq[0:DSA_ROPE].astype(F32) * scale).astype(BF16)

    def key_tile(kt):
        return keys_ref[0, pl.ds(pl.multiple_of(kt * T, T), T), :]

    w = iwt_ref[0]

    def score_tiles(first, g):
        for t in range(g):
            kt = first + t
            s = _dot(key_tile(kt), qit_ref[...])
            score = jnp.zeros((T, T), F32)
            for h in range(IDX_HEADS):
                score = score + jnp.maximum(s[:, T * h:T * (h + 1)], 0.0) * w[h:h + 1, :]
            rows = pl.ds(pl.multiple_of(kt * T, T), T)
            sc_ref[rows, :] = score
            kc_ref[rows, :] = score.astype(BF16)

    _for_tile_groups(n_tiles, score_tiles)
    r_hid = pl.multiple_of(j * T + CHUNK, CHUNK)
    key_chunk = (lax.broadcasted_iota(I32, (T - CHUNK, T), 0) >> CHUNK_SHIFT) + 1
    query_chunk = lax.broadcasted_iota(I32, (T - CHUNK, T), 1) >> CHUNK_SHIFT
    masked = jnp.where(key_chunk > query_chunk, -jnp.inf, sc_ref[pl.ds(r_hid, T - CHUNK), :])
    sc_ref[pl.ds(r_hid, T - CHUNK), :] = masked
    kc_ref[pl.ds(r_hid, T - CHUNK), :] = masked.astype(BF16)

    trows = T // SUBLANES
    full = lambda v: jnp.full((SUBLANES, T), v, I32)
    one_c = jnp.ones((PACKED_SUBLANES, T), BF16)
    zero_c = jnp.zeros((PACKED_SUBLANES, T), BF16)
    sub_total = lambda acc: jnp.broadcast_to(jnp.sum(acc, axis=0, keepdims=True), (SUBLANES, T))
    key_to_float = lambda key: lax.bitcast_convert_type(key ^ ((key >> 31) & INT_MAX), F32)
    thr_ref[...] = jnp.full((SUBLANES, T), F32_LOWEST, F32)
    cnt_ref[...] = full(n_sel)

    def descend(nt):
        def update(c, cand, cnt):
            thr, cnt_thr = c
            take = cnt >= n_sel
            return jnp.where(take, cand, thr), jnp.where(take, cnt, cnt_thr)

        def coarse_step(i, c):
            cand = c[0] + lax.shift_left(jnp.int32(1), 31 - i)
            hi = cand >> 16
            grid = lax.bitcast_convert_type((hi ^ ((hi >> 31) & 0x7FFF)) << 16, F32).astype(BF16)
            grid = jnp.broadcast_to(grid[0:1, :], (PACKED_SUBLANES, T))
            accs = [zero_c] * 4
            for idx in range(nt * T // PACKED_SUBLANES):
                row = PACKED_SUBLANES * idx
                accs[idx % 4] = accs[idx % 4] + jnp.where(kc_ref[row:row + PACKED_SUBLANES, :] >= grid, one_c, zero_c)
            total = (accs[0].astype(F32) + accs[1].astype(F32)) + (accs[2].astype(F32) + accs[3].astype(F32))
            cnt = jnp.sum(total, axis=0, keepdims=True).astype(I32)
            return update(c, cand, jnp.broadcast_to(cnt, (SUBLANES, T)))

        def fine_step(i, c):
            cand = c[0] + lax.shift_left(jnp.int32(1), 31 - i)
            cand_f = key_to_float(cand)
            accs = [full(0)] * 4
            for idx in range(nt * trows):
                row = SUBLANES * idx
                accs[idx % 4] = accs[idx % 4] + jnp.where(sc_ref[row:row + SUBLANES, :] >= cand_f, 1, 0)
            return update(c, cand, sub_total((accs[0] + accs[1]) + (accs[2] + accs[3])))

        start = (full(INT_MIN), full(nt * T))
        coarse, _ = lax.fori_loop(0, COARSE_BITS, coarse_step, start)
        base = jnp.where(coarse <= NEG_INF_KEY, INT_MIN, coarse - (1 << (32 - COARSE_BITS)))
        thr, cnt_thr = lax.fori_loop(COARSE_BITS - 2, 32, fine_step, (base, full(nt * T)))
        few = thr <= NEG_INF_KEY
        thr_ref[...] = jnp.where(few, F32_LOWEST, key_to_float(thr))
        cnt_ref[...] = jnp.where(few, n_sel, cnt_thr)

    for nt in range(n_sel // T + 1, n_qblk + 1):
        pl.when(n_tiles == nt)(functools.partial(descend, nt))

    @pl.when(jnp.max(jnp.where(cnt_ref[0:1, :] > n_sel, 1, 0)) > 0)
    def _demote_excess_ties():
        thr = thr_ref[...]
        excess = cnt_ref[...] - n_sel
        sub = lax.broadcasted_iota(I32, (SUBLANES, T), 0)

        def tile_body(i, later):
            rows = pl.ds(pl.multiple_of((n_tiles - 1 - i) * T, T), T)
            tile = sc_ref[rows, :]
            kept = [None] * trows
            for v in range(trows - 1, -1, -1):
                sc = tile[SUBLANES * v:SUBLANES * (v + 1), :]
                tied = jnp.where(sc == thr, 1, 0)
                after = tied
                for k in (1, 2, 4):
                    after = after + jnp.where(sub < SUBLANES - k, pltpu.roll(after, SUBLANES - k, 0), 0)
                rank_from_end = later + after - tied
                drop = jnp.where(tied > 0, jnp.where(rank_from_end < excess, 1, 0), 0)
                kept[v] = jnp.where(drop > 0, -jnp.inf, sc)
                later = later + jnp.broadcast_to(after[0:1, :], (SUBLANES, T))
            sc_ref[rows, :] = jnp.concatenate(kept, axis=0)
            return later

        lax.fori_loop(0, n_tiles, tile_body, full(0))

    thr_f = thr_ref[...]

    mx_ref[...] = jnp.full((SUBLANES, HT), -jnp.inf, F32)

    def logit_tiles(first, g):
        mx = mx_ref[...]
        for t in range(g):
            kt = first + t
            r0 = pl.multiple_of(kt * T, T)
            l = _dot(key_tile(kt), qt_ref[...])
            sc = sc_ref[pl.ds(r0, T), :].reshape(trows, SUBLANES, T)
            bias = jnp.where(sc >= thr_f[None], 0.0, -jnp.inf).reshape(T, T)
            new = []
            for h in range(DSA_HEADS):
                lm = l[:, T * h:T * (h + 1)] + bias
                lg_ref[pl.ds(r0, T), T * h:T * (h + 1)] = lm
                new.append(jnp.max(lm.reshape(trows, SUBLANES, T), axis=0))
            mx = jnp.maximum(mx, jnp.concatenate(new, axis=1))
        mx_ref[...] = mx

    _for_tile_groups(n_tiles, logit_tiles)
    mrow = jnp.broadcast_to(jnp.max(mx_ref[...], axis=0, keepdims=True), (SUBLANES, HT))

    acc_ref[...] = jnp.zeros(acc_ref.shape, F32)

    def pv_tiles(first, g):
        r0 = pl.multiple_of(first * T, T)
        ps = []
        for t in range(g):
            lm = lg_ref[pl.ds(pl.multiple_of((first + t) * T, T), T), :]
            p = jnp.exp2(lm.reshape(trows, SUBLANES, HT) - mrow[None])
            ps.append(p.reshape(T, HT).astype(BF16))
        p_all = ps[0] if g == 1 else jnp.concatenate(ps, axis=0)
        acc_ref[...] += _dot(ckvt_ref[0, :, pl.ds(r0, g * T)], p_all)

    _for_tile_groups(n_tiles, pv_tiles)
    linv = 1.0 / acc_ref[DSA_LATENT:DSA_LATENT + 1, :]
    olat = (acc_ref[0:DSA_LATENT, :] * linv).astype(BF16)
    outs = [_dot(wuvt_ref[h], olat[:, T * h:T * (h + 1)]) for h in range(DSA_HEADS)]
    o = jnp.concatenate(outs, axis=0).T
    o_ref[0] = (o * gate_ref[0].astype(F32)).astype(BF16)


def _dsa(keys, ckvt, iqt, dqt, iwt, w_uk, w_uv, dgate):
    B, S, _ = keys.shape
    T = DSA_QB
    assert S % T == 0 and T % CHUNK == 0
    n_qblk = S // T
    n_sel = min(TOPK_MAX, S // 4)
    H = DSA_HEADS
    wukt = jnp.concatenate([jnp.zeros((H, DSA_LATENT, DSA_ROPE), F32), jnp.swapaxes(w_uk, 1, 2)], axis=2).astype(BF16)
    wuvt = jnp.swapaxes(w_uv, 1, 2).astype(BF16)
    HT = H * T
    per_b = lambda b, j: (b, 0, 0)
    qcol = lambda b, j: (b, 0, j)
    full3 = lambda b, j: (0, 0, 0)
    return pl.pallas_call(
        functools.partial(_dsa_kernel, n_sel=n_sel, n_qblk=n_qblk),
        grid=(B, n_qblk),
        in_specs=[
            pl.BlockSpec((1, S, KEY_W), per_b),
            pl.BlockSpec((1, CKVT_ROWS, S), per_b),
            pl.BlockSpec((1, IDX_HEADS * IDX_DIM, T), qcol),
            pl.BlockSpec((1, H * DSA_HEAD_DIM, T), qcol),
            pl.BlockSpec((1, IDX_HEADS, T), qcol),
            pl.BlockSpec((H, DSA_LATENT, DSA_HEAD_DIM), full3),
            pl.BlockSpec((H, DSA_VDIM, DSA_LATENT), full3),
            pl.BlockSpec((1, T, H * DSA_VDIM), lambda b, j: (b, j, 0)),
        ],
        out_specs=pl.BlockSpec((1, T, H * DSA_VDIM), lambda b, j: (b, j, 0)),
        out_shape=jax.ShapeDtypeStruct((B, S, H * DSA_VDIM), BF16),
        scratch_shapes=[
            pltpu.VMEM((KEY_W, HT), BF16),
            pltpu.VMEM((KEY_W, HT), BF16),
            pltpu.VMEM((S, T), F32),
            pltpu.VMEM((S, T), BF16),
            pltpu.VMEM((S, HT), F32),
            pltpu.VMEM((CKVT_ROWS, HT), F32),
            pltpu.VMEM((SUBLANES, T), F32),
            pltpu.VMEM((SUBLANES, T), I32),
            pltpu.VMEM((SUBLANES, HT), F32),
        ],
        compiler_params=pltpu.CompilerParams(
            dimension_semantics=("arbitrary", "arbitrary"), vmem_limit_bytes=VMEM_LIMIT_BYTES),
        name="dsa",
    )(keys, ckvt, iqt, dqt, iwt, wukt, wuvt, dgate)


def _out_kernel(x_ref, ret_ref, dsa_ref, p_ref, wo_ref, gp_ref, wg_ref, wp_ref, gf_ref, o_ref, *, final_norm):
    nr = ret_ref.shape[2]
    x1 = x_ref[0] + _dot(ret_ref[0], wo_ref[0:nr, :]) + _dot(dsa_ref[0], wo_ref[nr:, :])
    ms = jnp.mean(x1 * x1, axis=-1, keepdims=True)
    h = (x1 * lax.rsqrt(ms + EPS) * gp_ref[...]).astype(BF16)
    z = _dot(h, wg_ref[...])
    gate = 1.0 / (1.0 + jnp.exp(-z))
    x2 = x1 + gate * _dot(p_ref[0].astype(BF16), wp_ref[...])
    if final_norm:
        ms2 = jnp.mean(x2 * x2, axis=-1, keepdims=True)
        x2 = x2 * lax.rsqrt(ms2 + EPS) * gf_ref[...]
    o_ref[0] = x2


def _out_proj(x, ret, dsa, p, w_out, norm_ple, w_gate, w_proj, norm_final, tm, final_norm):
    B, S, D = x.shape
    row = lambda b, s: (b, s, 0)
    full2 = lambda b, s: (0, 0)
    return pl.pallas_call(
        functools.partial(_out_kernel, final_norm=final_norm),
        grid=(B, S // tm),
        in_specs=[
            pl.BlockSpec((1, tm, D), row),
            pl.BlockSpec((1, tm, ret.shape[2]), row),
            pl.BlockSpec((1, tm, dsa.shape[2]), row),
            pl.BlockSpec((1, tm, p.shape[2]), row),
            pl.BlockSpec(w_out.shape, full2),
            pl.BlockSpec((1, D), full2),
            pl.BlockSpec(w_gate.shape, full2),
            pl.BlockSpec(w_proj.shape, full2),
            pl.BlockSpec((1, D), full2),
        ],
        out_specs=pl.BlockSpec((1, tm, D), row),
        out_shape=jax.ShapeDtypeStruct((B, S, D), F32),
        compiler_params=pltpu.CompilerParams(
            dimension_semantics=("arbitrary", "arbitrary"), vmem_limit_bytes=VMEM_LIMIT_BYTES),
        name="out_proj",
    )(x, ret, dsa, p, w_out.astype(BF16), norm_ple.reshape(1, D), w_gate.astype(BF16),
      w_proj.astype(BF16), norm_final.reshape(1, D))


def _tiles(S):
    tm = math.gcd(S, 512)
    blk = math.gcd(S, 256)
    return tm, blk


def kernel(x, p, positions, norm_mix, w_in, ret_norm, kv_norm, w_uk, w_uv, w_out, norm_ple, w_ple_gate, w_ple_proj, norm_final):
    B, S, D = x.shape
    depth = w_in.shape[0]
    tm, blk = _tiles(S)
    assert blk % CHUNK == 0 and (1 << CHUNK_SHIFT) == CHUNK
    for i in range(depth):
        rq, rk, rv, rg, dg, keys, ckvt, iqt, dqt, iwt = _in_proj(x, positions, norm_mix[i], w_in[i], kv_norm[i], tm)
        ret = _retention(rq, rk, rv, rg, ret_norm[i], blk)
        dsa = _dsa(keys, ckvt, iqt, dqt, iwt, w_uk[i], w_uv[i], dg)
        x = _out_proj(x, ret, dsa, p[i], w_out[i], norm_ple[i], w_ple_gate[i], w_ple_proj[i],
                      norm_final, tm, final_norm=(i == depth - 1))
    return x
```
